```python
import jax, jax.numpy as jnp
from jax import lax
import numpy as np

D_MODEL = 1024
BATCH = 8
SEQ = 4096
DEPTH = 2

HEAD_DIM = 64
A_HEADS = 6
DILATED_PATTERNS = ((128, 1), (512, 4), (2048, 16))
CONV_CH = 256
CONV_K = 3
C_Q_HEADS = 6
C_KV_HEADS = 2
C_GROUP = C_Q_HEADS // C_KV_HEADS
C_WINDOW = 128
BLOCK = 128
D_FF = 4 * D_MODEL
EPS = 1e-6

A_WIDTH = A_HEADS * HEAD_DIM
C_WIDTH = C_Q_HEADS * HEAD_DIM
KV_WIDTH = C_KV_HEADS * HEAD_DIM
MIX_WIDTH = A_WIDTH + CONV_CH + C_WIDTH
IN_SPLITS = (A_WIDTH, A_WIDTH, A_WIDTH, CONV_CH, CONV_CH, CONV_CH, C_WIDTH, KV_WIDTH, KV_WIDTH)
IN_WIDTH = sum(IN_SPLITS)
SPLIT_POINTS = tuple(int(p) for p in np.cumsum(IN_SPLITS)[:-1])

kernel_name = "hymba_dilated_conv_swa_sink_trunk"


def rms_normalize(t):
    t32 = t.astype(jnp.float32)
    return (t32 * lax.rsqrt(jnp.mean(t32 * t32, axis=-1, keepdims=True) + EPS)).astype(t.dtype)


def rmsnorm(t, g):
    return rms_normalize(t) * g


def banded_attention(q, k, v, max_dist, sink_logits=None):
    n, L, hkv, g, dh = q.shape
    nb = -(-L // BLOCK)
    lp = nb * BLOCK
    pad = lp - L
    q = jnp.pad(q, ((0, 0), (0, pad), (0, 0), (0, 0), (0, 0)))
    kv_pad = ((0, 0), (BLOCK, pad), (0, 0), (0, 0))
    k = jnp.pad(k, kv_pad).reshape(n, nb + 1, BLOCK, hkv, dh)
    v = jnp.pad(v, kv_pad).reshape(n, nb + 1, BLOCK, hkv, dh)
    k2 = jnp.concatenate([k[:, :-1], k[:, 1:]], axis=2)
    v2 = jnp.concatenate([v[:, :-1], v[:, 1:]], axis=2)
    qb = q.reshape(n, nb, BLOCK, hkv, g, dh)
    s = jnp.einsum('nbqhgd,nbkhd->nbhgqk', qb, k2).astype(jnp.float32) * (dh ** -0.5)
    qi = jnp.arange(BLOCK)[:, None]
    kj = jnp.arange(2 * BLOCK)[None, :]
    dist = BLOCK + qi - kj
    band = (dist >= 0) & (dist <= max_dist)
    first = jnp.arange(nb)[:, None, None] == 0
    mask = band[None] & ~(first & (kj < BLOCK)[None])
    s = jnp.where(mask[None, :, None, None], s, -jnp.inf)
    if sink_logits is not None:
        sink = jnp.broadcast_to(sink_logits.astype(jnp.float32)[None, None, :, :, None, None], s.shape[:-1] + (1,))
        lse = jax.nn.logsumexp(jnp.concatenate([s, sink], axis=-1), axis=-1)
    else:
        lse = jax.nn.logsumexp(s, axis=-1)
    p = jnp.exp(s - lse[..., None]).astype(v2.dtype)
    o = jnp.einsum('nbhgqk,nbkhd->nbqhgd', p, v2).reshape(n, lp, hkv, g, dh)[:, :L]
    lse = lse.transpose(0, 1, 4, 2, 3).reshape(n, lp, hkv, g)[:, :L]
    return o, lse


def to_residues(t, dil):
    b, s, h, dh = t.shape
    return t.reshape(b, s // dil, dil, h, dh).transpose(0, 2, 1, 3, 4).reshape(b * dil, s // dil, h, dh)


def from_residues(t, dil, b):
    sub = t.shape[1]
    rest = t.shape[2:]
    t = t.reshape((b, dil, sub) + rest)
    t = jnp.moveaxis(t, 1, 2)
    return t.reshape((b, sub * dil) + rest)


def dilated_attention(q, k, v):
    b = q.shape[0]
    outs, lses = [], []
    for window, dil in DILATED_PATTERNS:
        o, lse = banded_attention(to_residues(q, dil)[:, :, :, None], to_residues(k, dil),
                                  to_residues(v, dil), window // dil)
        outs.append(from_residues(o[:, :, :, 0], dil, b))
        lses.append(from_residues(lse[..., 0], dil, b))
    wts = jax.nn.softmax(jnp.stack(lses), axis=0)
    return jnp.einsum('pbsh,pbshd->bshd', wts.astype(q.dtype), jnp.stack(outs))


def short_gated_conv(gate_b, gate_c, xb, w):
    s = xb.shape[1]
    u = gate_c * xb
    up = jnp.pad(u, ((0, 0), (CONV_K - 1, 0), (0, 0)))
    y = sum(w[i] * up[:, i:i + s] for i in range(CONV_K))
    return gate_b * y


def setup_inputs(seed: int = 0) -> dict:
    key = jax.random.key(seed)
    ks = jax.random.split(key, 12)
    nrm = jax.random.normal
    x = nrm(ks[0], (BATCH, SEQ, D_MODEL), jnp.float32)
    w_in = nrm(ks[1], (DEPTH, D_MODEL, IN_WIDTH), jnp.float32) * D_MODEL ** -0.5
    conv_w = nrm(ks[2], (DEPTH, CONV_K, CONV_CH), jnp.float32) * CONV_K ** -0.5
    sinks = nrm(ks[3], (DEPTH, C_KV_HEADS, C_GROUP), jnp.float32) * 0.5
    g_mix = 1.0 + 0.02 * nrm(ks[4], (DEPTH, D_MODEL), jnp.float32)
    g_group = 1.0 + 0.02 * nrm(ks[5], (DEPTH, MIX_WIDTH), jnp.float32)
    w_o = nrm(ks[6], (DEPTH, MIX_WIDTH, D_MODEL), jnp.float32) * MIX_WIDTH ** -0.5
    g_mlp = 1.0 + 0.02 * nrm(ks[7], (DEPTH, D_MODEL), jnp.float32)
    w_ff_in = nrm(ks[8], (DEPTH, D_MODEL, D_FF), jnp.float32) * D_MODEL ** -0.5
    w_ff_out = nrm(ks[9], (DEPTH, D_FF, D_MODEL), jnp.float32) * D_FF ** -0.5
    g_final = 1.0 + 0.02 * nrm(ks[10], (D_MODEL,), jnp.float32)
    return {"x": x, "w_in": w_in, "conv_w": conv_w, "sinks": sinks, "g_mix": g_mix,
            "g_group": g_group, "w_o": w_o, "g_mlp": g_mlp, "w_ff_in": w_ff_in,
            "w_ff_out": w_ff_out, "g_final": g_final}


def reference(x, w_in, conv_w, sinks, g_mix, g_group, w_o, g_mlp, w_ff_in, w_ff_out, g_final):
    b, s, _ = x.shape
    for l in range(DEPTH):
        h = rmsnorm(x, g_mix[l])
        z = jnp.einsum('bsd,de->bse', h, w_in[l])
        qa, ka, va, gb, gc, xb, qc, kc, vc = jnp.split(z, SPLIT_POINTS, axis=-1)
        ya = dilated_attention(qa.reshape(b, s, A_HEADS, HEAD_DIM),
                               ka.reshape(b, s, A_HEADS, HEAD_DIM),
                               va.reshape(b, s, A_HEADS, HEAD_DIM)).reshape(b, s, A_WIDTH)
        yb = short_gated_conv(gb, gc, xb, conv_w[l])
        oc, _ = banded_attention(qc.reshape(b, s, C_KV_HEADS, C_GROUP, HEAD_DIM),
                                 kc.reshape(b, s, C_KV_HEADS, HEAD_DIM),
                                 vc.reshape(b, s, C_KV_HEADS, HEAD_DIM),
                                 C_WINDOW - 1, sinks[l])
        yc = oc.reshape(b, s, C_WIDTH)
        y = jnp.concatenate([rms_normalize(ya), rms_normalize(yb), rms_normalize(yc)], axis=-1) * g_group[l]
        x = x + jnp.einsum('bse,ed->bsd', y, w_o[l])
        h2 = rmsnorm(x, g_mlp[l])
        a = jnp.square(jax.nn.relu(jnp.einsum('bsd,df->bsf', h2, w_ff_in[l])))
        x = x + jnp.einsum('bsf,fd->bsd', a, w_ff_out[l])
    return rmsnorm(x, g_final)
```

```python
import functools

import jax
import jax.numpy as jnp
from jax import lax
from jax.experimental import pallas as pl
from jax.experimental.pallas import tpu as pltpu

D_MODEL = 1024
HEAD_DIM = 64
A_HEADS = 6
DILATIONS = (1, 4, 16)
A_MAX_DIST = 128
CONV_CH = 256
CONV_K = 3
C_Q_HEADS = 6
C_KV_HEADS = 2
C_GROUP = C_Q_HEADS // C_KV_HEADS
C_MAX_DIST = 127
BLOCK = 128
D_FF = 4 * D_MODEL
EPS = 1e-6

A_WIDTH = A_HEADS * HEAD_DIM
C_WIDTH = C_Q_HEADS * HEAD_DIM
KV_WIDTH = C_KV_HEADS * HEAD_DIM
MIX_WIDTH = A_WIDTH + CONV_CH + C_WIDTH
IN_SPLITS = (A_WIDTH, A_WIDTH, A_WIDTH, CONV_CH, CONV_CH, CONV_CH, C_WIDTH, KV_WIDTH, KV_WIDTH)
IN_WIDTH = sum(IN_SPLITS)
LANES = 128

PROJ_TOKENS = 512
MLP_TOKENS = 256
FF_CHUNK = 1024
VMEM_LIMIT = 56 * 1024 * 1024

NEG = -1e30
BF16 = jnp.bfloat16
F32 = jnp.float32


def _rms_normalize(t):
    return t * lax.rsqrt(jnp.mean(t * t, axis=-1, keepdims=True) + EPS)


def _dot(a, b):
    return jnp.dot(a, b, preferred_element_type=F32)


def _dot_nt(a, b):
    return lax.dot_general(a, b, (((1,), (1,)), ((), ())), preferred_element_type=F32)


def _in_proj_kernel(x_ref, g_ref, w_ref, *out_refs):
    h = (_rms_normalize(x_ref[...]) * g_ref[...]).astype(BF16)
    start = 0
    for width, o_ref in zip(IN_SPLITS, out_refs):
        o_ref[...] = _dot(h, w_ref[:, start:start + width]).astype(o_ref.dtype)
        start += width


def _in_proj(x2d, g, w):
    n = x2d.shape[0]
    tm = PROJ_TOKENS
    return pl.pallas_call(
        _in_proj_kernel,
        grid=(n // tm,),
        in_specs=[
            pl.BlockSpec((tm, D_MODEL), lambda i: (i, 0)),
            pl.BlockSpec((1, D_MODEL), lambda i: (0, 0)),
            pl.BlockSpec((D_MODEL, IN_WIDTH), lambda i: (0, 0)),
        ],
        out_specs=[pl.BlockSpec((tm, wd), lambda i: (i, 0)) for wd in IN_SPLITS],
        out_shape=[jax.ShapeDtypeStruct((n, wd), BF16) for wd in IN_SPLITS],
        compiler_params=pltpu.CompilerParams(
            dimension_semantics=("parallel",), vmem_limit_bytes=VMEM_LIMIT),
        name="in_proj",
    )(x2d, g, w)


def _band_masks(max_dist, groups, not_first):
    qi = lax.broadcasted_iota(jnp.int32, (groups * BLOCK, BLOCK), 0) % BLOCK
    kj = lax.broadcasted_iota(jnp.int32, (groups * BLOCK, BLOCK), 1)
    limit = jnp.where(not_first, max_dist, -1)
    prev = (kj >= qi) & (BLOCK + qi - kj <= limit)
    cur = kj <= qi
    return prev, cur


def _banded_head(q, kp, kc, vp, vc, prev_mask, cur_mask, sink=None):
    sp = jnp.where(prev_mask, _dot_nt(q, kp), NEG)
    sc = jnp.where(cur_mask, _dot_nt(q, kc), NEG)
    m = jnp.maximum(jnp.max(sp, axis=-1, keepdims=True), jnp.max(sc, axis=-1, keepdims=True))
    if sink is not None:
        m = jnp.maximum(m, sink)
    pp = jnp.exp(sp - m)
    pc = jnp.exp(sc - m)
    l = jnp.sum(pp, axis=-1, keepdims=True) + jnp.sum(pc, axis=-1, keepdims=True)
    if sink is not None:
        l = l + jnp.exp(sink - m)
    o = _dot(pp.astype(BF16), vp) + _dot(pc.astype(BF16), vc)
    return o / l, m + jnp.log(l)


def _dilated_attn_kernel(q_ref, kp_ref, kc_ref, vp_ref, vc_ref, o_ref, lse_ref):
    not_first = pl.program_id(1) > 0
    prev_mask, cur_mask = _band_masks(A_MAX_DIST, 1, not_first)
    lane =lax.broadcasted_iota(jnp.int32, (BLOCK, LANES), 1)
    lse_tile = jnp.zeros((BLOCK, LANES), F32)
    for h in range(A_HEADS):
        cols = slice(h * HEAD_DIM, (h + 1) * HEAD_DIM)
        o, lse = _banded_head(q_ref[0, :, cols], kp_ref[0, :, cols], kc_ref[0, :, cols],
                              vp_ref[0, :, cols], vc_ref[0, :, cols], prev_mask, cur_mask)
        o_ref[0, :, cols] = o.astype(o_ref.dtype)
        lse_tile = jnp.where(lane == h, lse, lse_tile)
    lse_ref[0] = lse_tile


def _dilated_attn(q, k, v):
    n_seq, length, _ = q.shape
    cur = pl.BlockSpec((1, BLOCK, A_WIDTH), lambda s, i: (s, i, 0))
    prev = pl.BlockSpec((1, BLOCK, A_WIDTH), lambda s, i: (s, jnp.maximum(i - 1, 0), 0))
    return pl.pallas_call(
        _dilated_attn_kernel,
        grid=(n_seq, length // BLOCK),
        in_specs=[cur, prev, cur, prev, cur],
        out_specs=[cur, pl.BlockSpec((1, BLOCK, LANES), lambda s, i: (s, i, 0))],
        out_shape=[jax.ShapeDtypeStruct((n_seq, length, A_WIDTH), BF16),
                   jax.ShapeDtypeStruct((n_seq, length, LANES), F32)],
        compiler_params=pltpu.CompilerParams(
            dimension_semantics=("parallel", "arbitrary"), vmem_limit_bytes=VMEM_LIMIT),
        name="dilated_attn",
    )(q, k, k, v, v)


def _swa_conv_kernel(sink_ref, q_ref, kp_ref, kc_ref, vp_ref, vc_ref,
                     gb_ref, gcp_ref, gcc_ref, xbp_ref, xbc_ref, cw_ref, gg_b_ref, gg_c_ref,
                     yb_ref, yc_ref, oc_scr):
    not_first = pl.program_id(1) > 0

    prev_mask, cur_mask = _band_masks(C_MAX_DIST, C_GROUP, not_first)
    grp = lax.broadcasted_iota(jnp.int32, (C_GROUP * BLOCK, 1), 0) // BLOCK
    for hk in range(C_KV_HEADS):
        q = jnp.concatenate(
            [q_ref[0, :, (hk * C_GROUP + g) * HEAD_DIM:(hk * C_GROUP + g + 1) * HEAD_DIM]
             for g in range(C_GROUP)], axis=0)
        sink = jnp.zeros((C_GROUP * BLOCK, 1), F32)
        for g in range(C_GROUP):
            sink = jnp.where(grp == g, sink_ref[hk * C_GROUP + g], sink)
        cols = slice(hk * HEAD_DIM, (hk + 1) * HEAD_DIM)
        o, _ = _banded_head(q, kp_ref[0, :, cols], kc_ref[0, :, cols],
                            vp_ref[0, :, cols], vc_ref[0, :, cols], prev_mask, cur_mask, sink)
        for g in range(C_GROUP):
            oc_scr[:, (hk * C_GROUP + g) * HEAD_DIM:(hk * C_GROUP + g + 1) * HEAD_DIM] = (
                o[g * BLOCK:(g + 1) * BLOCK])
    yc_ref[0] = (_rms_normalize(oc_scr[...]) * gg_c_ref[...]).astype(yc_ref.dtype)

    u_cur = gcc_ref[0].astype(F32) * xbc_ref[0].astype(F32)
    u_prev = jnp.where(not_first, gcp_ref[0].astype(F32) * xbp_ref[0].astype(F32), 0.0)
    row = lax.broadcasted_iota(jnp.int32, (BLOCK, CONV_CH), 0)
    acc = cw_ref[CONV_K - 1:CONV_K, :] * u_cur
    for shift in range(1, CONV_K):
        shifted = jnp.where(row < shift, pltpu.roll(u_prev, shift, 0), pltpu.roll(u_cur, shift, 0))
        acc = acc + cw_ref[CONV_K - 1 - shift:CONV_K - shift, :] * shifted
    yb = gb_ref[0].astype(F32) * acc
    yb_ref[0] = (_rms_normalize(yb) * gg_b_ref[...]).astype(yb_ref.dtype)


def _swa_conv(sinks, qc, kc, vc, gb, gc, xb, conv_w, gg_b, gg_c):
    batch, seq, _ = qc.shape

    def cur(width):
        return pl.BlockSpec((1, BLOCK, width), lambda b, i: (b, i, 0))

    def prev(width):
        return pl.BlockSpec((1, BLOCK, width), lambda b, i: (b, jnp.maximum(i - 1, 0), 0))

    def const(shape):
        return pl.BlockSpec(shape, lambda b, i: (0, 0))

    return pl.pallas_call(
        _swa_conv_kernel,
        grid=(batch, seq // BLOCK),
        in_specs=[
            pl.BlockSpec(memory_space=pltpu.SMEM),
            cur(C_WIDTH), prev(KV_WIDTH), cur(KV_WIDTH), prev(KV_WIDTH), cur(KV_WIDTH),
            cur(CONV_CH), prev(CONV_CH), cur(CONV_CH), prev(CONV_CH), cur(CONV_CH),
            const((CONV_K, CONV_CH)), const((1, CONV_CH)), const((1, C_WIDTH)),
        ],
        out_specs=[cur(CONV_CH), cur(C_WIDTH)],
        out_shape=[jax.ShapeDtypeStruct((batch, seq, CONV_CH), BF16),
                   jax.ShapeDtypeStruct((batch, seq, C_WIDTH), BF16)],
        scratch_shapes=[pltpu.VMEM((BLOCK, C_WIDTH), F32)],
        compiler_params=pltpu.CompilerParams(
            dimension_semantics=("parallel", "arbitrary"), vmem_limit_bytes=VMEM_LIMIT),
        name="swa_conv",
    )(sinks, qc, kc, kc, vc, vc, gb, gc, gc, xb, xb, conv_w, gg_b, gg_c)


def _expand_heads(w):
    head_of_col = lax.broadcasted_iota(jnp.int32, (LANES, A_WIDTH), 1) // HEAD_DIM
    lane = lax.broadcasted_iota(jnp.int32, (LANES, A_WIDTH), 0)
    expand = (head_of_col == lane).astype(BF16)
    hi = w.astype(BF16)
    lo = (w - hi.astype(F32)).astype(BF16)
    return _dot(hi, expand) + _dot(lo, expand)


def _mix_mlp_kernel(x_ref, o1_ref, o2_ref, o3_ref, l1_ref, l2_ref, l3_ref, yb_ref, yc_ref,
                    gg_a_ref, wo_ref, g_mlp_ref, w1_ref, w2_ref, g_fin_ref, out_ref, *, final):
    lses = (l1_ref[...], l2_ref[...], l3_ref[...])
    top = jnp.maximum(jnp.maximum(lses[0], lses[1]), lses[2])
    es = [jnp.exp(l - top) for l in lses]
    inv = 1.0 / (es[0] + es[1] + es[2])
    ya = jnp.zeros((x_ref.shape[0], A_WIDTH), F32)
    for e, o_ref in zip(es, (o1_ref, o2_ref, o3_ref)):
        ya = ya + _expand_heads(e * inv) * o_ref[...].astype(F32)
    ya = (_rms_normalize(ya) * gg_a_ref[...]).astype(BF16)

    mixed = (_dot(ya, wo_ref[0:A_WIDTH, :])
             + _dot(yb_ref[...], wo_ref[A_WIDTH:A_WIDTH + CONV_CH, :])
             + _dot(yc_ref[...], wo_ref[A_WIDTH + CONV_CH:MIX_WIDTH, :]))
    x = x_ref[...] + mixed

    h = (_rms_normalize(x) * g_mlp_ref[...]).astype(BF16)
    ff = jnp.zeros_like(x)
    for c in range(D_FF // FF_CHUNK):
        a = jnp.square(jnp.maximum(_dot(h, w1_ref[:, c * FF_CHUNK:(c + 1) * FF_CHUNK]), 0.0))
        ff = ff + _dot(a.astype(BF16), w2_ref[c * FF_CHUNK:(c + 1) * FF_CHUNK, :])
    x = x + ff
    if final:
        x = _rms_normalize(x) * g_fin_ref[...]
    out_ref[...] = x


def _mix_mlp(x2d, o1, o2, o3, l1, l2, l3, yb, yc, gg_a, wo, g_mlp, w1, w2, g_fin, final):
    n = x2d.shape[0]
    tm = MLP_TOKENS

    def rows(width):
        return pl.BlockSpec((tm, width), lambda i: (i, 0))

    def const(shape):
        return pl.BlockSpec(shape, lambda i: (0, 0), pipeline_mode=pl.Buffered(1))

    return pl.pallas_call(
        functools.partial(_mix_mlp_kernel, final=final),
        grid=(n // tm,),
        in_specs=[
            rows(D_MODEL), rows(A_WIDTH), rows(A_WIDTH), rows(A_WIDTH),
            rows(LANES), rows(LANES), rows(LANES), rows(CONV_CH), rows(C_WIDTH),
            const((1, A_WIDTH)), const((MIX_WIDTH, D_MODEL)), const((1, D_MODEL)),
            const((D_MODEL, D_FF)), const((D_FF, D_MODEL)), const((1, D_MODEL)),
        ],
        out_specs=rows(D_MODEL),
        out_shape=jax.ShapeDtypeStruct((n, D_MODEL), F32),
        compiler_params=pltpu.CompilerParams(
            dimension_semantics=("parallel",), vmem_limit_bytes=VMEM_LIMIT),
        name="mix_mlp",
    )(x2d, o1, o2, o3, l1, l2, l3, yb, yc, gg_a, wo, g_mlp, w1, w2, g_fin)


def _to_residues(t, batch, seq, dil):
    w = t.shape[-1]
    return t.reshape(batch, seq // dil, dil, w).transpose(0, 2, 1, 3).reshape(batch * dil, seq // dil, w)


def _from_residues(t, batch, seq, dil):
    w = t.shape[-1]
    return t.reshape(batch, dil, seq // dil, w).transpose(0, 2, 1, 3).reshape(batch * seq, w)


def kernel(x, w_in, conv_w, sinks, g_mix, g_group, w_o, g_mlp, w_ff_in, w_ff_out, g_final):
    batch, seq, _ = x.shape
    depth = w_in.shape[0]
    n = batch * seq
    q_scale = jnp.ones((IN_WIDTH,), F32)
    q_scale = q_scale.at[0:A_WIDTH].set(HEAD_DIM ** -0.5)
    qc_start = sum(IN_SPLITS[:6])
    q_scale = q_scale.at[qc_start:qc_start + C_WIDTH].set(HEAD_DIM ** -0.5)

    x2d = x.reshape(n, D_MODEL)
    for l in range(depth):
        w_in_l = (w_in[l] * q_scale).astype(BF16)
        qa, ka, va, gb, gc, xb, qc, kc, vc = _in_proj(x2d, g_mix[l].reshape(1, D_MODEL), w_in_l)

        outs, lses = [], []
        for dil in DILATIONS:
            o, lse = _dilated_attn(_to_residues(qa, batch, seq, dil), _to_residues(ka, batch, seq, dil),
                                   _to_residues(va, batch, seq, dil))
            outs.append(_from_residues(o, batch, seq, dil))
            lses.append(_from_residues(lse, batch, seq, dil))

        def seqs(t):
            return t.reshape(batch, seq, t.shape[-1])

        gg = g_group[l]
        yb, yc = _swa_conv(sinks[l].reshape(C_Q_HEADS), seqs(qc), seqs(kc), seqs(vc),
                           seqs(gb), seqs(gc), seqs(xb), conv_w[l],
                           gg[A_WIDTH:A_WIDTH + CONV_CH].reshape(1, CONV_CH),
                           gg[A_WIDTH + CONV_CH:].reshape(1, C_WIDTH))

        x2d = _mix_mlp(x2d, outs[0], outs[1], outs[2], lses[0], lses[1], lses[2],
                       yb.reshape(n, CONV_CH), yc.reshape(n, C_WIDTH),
                       gg[:A_WIDTH].reshape(1, A_WIDTH), w_o[l].astype(BF16),
                       g_mlp[l].reshape(1, D_MODEL), w_ff_in[l].astype(BF16), w_ff_out[l].astype(BF16),
                       g_final.reshape(1, D_MODEL), final=(l == depth - 1))
    return x2d.reshape(batch, seq, D_MODEL)
```

```python
import functools

import jax
import jax.numpy as jnp
import numpy as np
from jax import lax
from jax.experimental import pallas as pl
from jax.experimental.pallas import tpu as pltpu

D_MODEL = 1024
HEAD_DIM = 64
A_HEADS = 6
DILATIONS = (1, 4, 16)
A_MAX_DIST = 128
CONV_CH = 256
CONV_K = 3
C_Q_HEADS = 6
C_KV_HEADS = 2
C_GROUP = C_Q_HEADS // C_KV_HEADS
C_MAX_DIST = 127
BLOCK = 128
D_FF = 4 * D_MODEL
EPS = 1e-6

A_WIDTH = A_HEADS * HEAD_DIM
C_WIDTH = C_Q_HEADS * HEAD_DIM
KV_WIDTH = C_KV_HEADS * HEAD_DIM
MIX_WIDTH = A_WIDTH + CONV_CH + C_WIDTH
IN_SPLITS = (A_WIDTH, A_WIDTH, A_WIDTH, CONV_CH, CONV_CH, CONV_CH, C_WIDTH, KV_WIDTH, KV_WIDTH)
IN_WIDTH = sum(IN_SPLITS)
LANES = 128
SUBLANES = 8
CONV_TAIL = 16
C_HEAD_ORDER = (0, 3, 1, 4, 2, 5)
ATTN_BLOCKS = 4

PROJ_TOKENS = 512
MLP_TOKENS = 256
FF_CHUNK = 1024
VMEM_LIMIT = 56 * 1024 * 1024

NEG = -1e30
BF16 = jnp.bfloat16
F32 = jnp.float32


def _rms_normalize(t):
    return t * lax.rsqrt(jnp.mean(t * t, axis=-1, keepdims=True) + EPS)


def _dot(a, b):
    return jnp.dot(a, b, preferred_element_type=F32)


def _dot_nt(a, b):
    return lax.dot_general(a, b, (((1,), (1,)), ((), ())), preferred_element_type=F32)


def _in_proj_kernel(x_ref, g_ref, w_ref, *out_refs):
    h = (_rms_normalize(x_ref[...]) * g_ref[...]).astype(BF16)
    start = 0
    for width, o_ref in zip(IN_SPLITS, out_refs):
        o_ref[...] = _dot(h, w_ref[:, start:start + width]).astype(o_ref.dtype)
        start += width


def _in_proj(x2d, g, w):
    n = x2d.shape[0]
    tm = PROJ_TOKENS
    return pl.pallas_call(
        _in_proj_kernel,
        grid=(n // tm,),
        in_specs=[
            pl.BlockSpec((tm, D_MODEL), lambda i: (i, 0)),
            pl.BlockSpec((1, D_MODEL), lambda i: (0, 0)),
            pl.BlockSpec((D_MODEL, IN_WIDTH), lambda i: (0, 0)),
        ],
        out_specs=[pl.BlockSpec((tm, wd), lambda i: (i, 0)) for wd in IN_SPLITS],
        out_shape=[jax.ShapeDtypeStruct((n, wd), BF16) for wd in IN_SPLITS],
        compiler_params=pltpu.CompilerParams(
            dimension_semantics=("parallel",), vmem_limit_bytes=VMEM_LIMIT),
        name="in_proj",
    )(x2d, g, w)


def _band_bias(max_dist):
    qi = np.arange(2 * BLOCK)[:, None] % BLOCK
    kj = np.arange(2 * BLOCK)[None, :]
    dist = BLOCK + qi - kj
    band = (dist >= 0) & (dist <= max_dist)
    first = band & (kj >= BLOCK)
    return jnp.asarray(np.where(np.stack([first, band]), 0.0, NEG), F32)


def _low_half():
    return lax.broadcasted_iota(jnp.int32, (BLOCK, LANES), 1) < HEAD_DIM


def _paired_attention(qs, ks, vs, bias, sinks=None):
    low = _low_half()
    ones = jnp.ones((2 * BLOCK, LANES), BF16)
    scores = []
    for q2, k2 in zip(qs, ks):
        zero = jnp.zeros_like(q2)
        stacked = jnp.concatenate([jnp.where(low, q2, zero), jnp.where(low, zero, q2)], axis=0)
        scores.append(_dot_nt(stacked, k2) + bias)
    probs, tops = [], []
    for g, s in enumerate(scores):
        m = jnp.max(s, axis=-1, keepdims=True)
        if sinks is not None:
            m = jnp.maximum(m, sinks[g])
        probs.append(jnp.exp(s - m).astype(BF16))
        tops.append(m)
    raw = [_dot(p, jnp.concatenate([v2, ones], axis=1)) for p, v2 in zip(probs, vs)]
    outs, denoms = [], []
    for g, r in enumerate(raw):
        denom = r[:, LANES:]
        if sinks is not None:
            denom = denom + jnp.exp(sinks[g] - tops[g])
        o = r[:, :LANES] / denom
        outs.append(jnp.where(low, o[:BLOCK], o[BLOCK:]))
        denoms.append(denom)
    return outs, tops, denoms


def _block_rows(jb):
    return slice(jb * BLOCK, (jb + 1) * BLOCK)


def _keys_of_block(prev_ref, cur_ref, jb, cols):
    before = prev_ref[0, :, cols] if jb == 0 else cur_ref[0, _block_rows(jb - 1), cols]
    return jnp.concatenate([before, cur_ref[0, _block_rows(jb), cols]], axis=0)


def _dilated_attn_kernel(bias_ref, q_ref, kp_ref, kc_ref, vp_ref, vc_ref, o_ref, lse_ref, *, blocks):
    lane = lax.broadcasted_iota(jnp.int32, (BLOCK, LANES), 1)
    groups = [slice(g * LANES, (g + 1) * LANES) for g in range(A_WIDTH // LANES)]
    for jb in range(blocks):
        bias = bias_ref[jnp.minimum(pl.program_id(1), 1)] if jb == 0 else bias_ref[1]
        rows = _block_rows(jb)
        outs, tops, denoms = _paired_attention(
            [q_ref[0, rows, cols] for cols in groups],
            [_keys_of_block(kp_ref, kc_ref, jb, cols) for cols in groups],
            [_keys_of_block(vp_ref, vc_ref, jb, cols) for cols in groups], bias)
        lse_tile = jnp.zeros((BLOCK, LANES), F32)
        for g, cols in enumerate(groups):
            o_ref[0, rows, cols] = outs[g].astype(o_ref.dtype)
            lse = tops[g] + jnp.log(denoms[g])
            lse_tile = jnp.where(lane == 2 * g, lse[:BLOCK], lse_tile)
            lse_tile = jnp.where(lane == 2 * g + 1, lse[BLOCK:], lse_tile)
        lse_ref[0, rows, :] = lse_tile


def _dilated_attn(q, k, v):
    n_seq, length, _ = q.shape
    blocks = min(ATTN_BLOCKS, length // BLOCK)
    cur = pl.BlockSpec((1, blocks * BLOCK, A_WIDTH), lambda s, i: (s, i, 0))
    prev = pl.BlockSpec((1, BLOCK, A_WIDTH), lambda s, i: (s, jnp.maximum(i * blocks - 1, 0), 0))
    return pl.pallas_call(
        functools.partial(_dilated_attn_kernel, blocks=blocks),
        grid=(n_seq, length // (blocks * BLOCK)),
        in_specs=[pl.BlockSpec((2, 2 * BLOCK, 2 * BLOCK), lambda s, i: (0, 0, 0)),
                  cur, prev, cur, prev, cur],
        out_specs=[cur, pl.BlockSpec((1, blocks * BLOCK, LANES), lambda s, i: (s, i, 0))],
        out_shape=[jax.ShapeDtypeStruct((n_seq, length, A_WIDTH), BF16),
                   jax.ShapeDtypeStruct((n_seq, length, LANES), F32)],
        compiler_params=pltpu.CompilerParams(
            dimension_semantics=("parallel", "arbitrary"), vmem_limit_bytes=VMEM_LIMIT),
        name="dilated_attn",
    )(_band_bias(A_MAX_DIST), q, k, k, v, v)


def _swa_conv_kernel(sink_ref, bias_ref, q_ref, kp_ref, kc_ref, vp_ref, vc_ref,
                     gb_ref, gcp_ref, gcc_ref, xbp_ref, xbc_ref, cw_ref, gg_b_ref, gg_c_ref,
                     yb_ref, yc_ref, *, blocks):
    not_first = pl.program_id(1) > 0
    groups = [slice(g * LANES, (g + 1) * LANES) for g in range(C_WIDTH // LANES)]
    upper = lax.broadcasted_iota(jnp.int32, (2 * BLOCK, 1), 0) < BLOCK
    sinks = [jnp.where(upper, sink_ref[C_HEAD_ORDER[2 * g]], sink_ref[C_HEAD_ORDER[2 * g + 1]])
             for g in range(len(groups))]
    everything = slice(0, KV_WIDTH)
    for jb in range(blocks):
        bias = bias_ref[jnp.minimum(pl.program_id(1), 1)] if jb == 0 else bias_ref[1]
        rows = _block_rows(jb)
        k2 = _keys_of_block(kp_ref, kc_ref, jb, everything)
        v2 = _keys_of_block(vp_ref, vc_ref, jb, everything)
        outs, _, _ = _paired_attention([q_ref[0, rows, cols] for cols in groups],
                                       [k2] * len(groups), [v2] * len(groups), bias, sinks)
        oc = jnp.concatenate(outs, axis=1)
        yc_ref[0, rows, :] = (_rms_normalize(oc) * gg_c_ref[...]).astype(yc_ref.dtype)

    u_cur = gcc_ref[0].astype(F32) * xbc_ref[0].astype(F32)
    u_prev = jnp.where(not_first, gcp_ref[0].astype(F32) * xbp_ref[0].astype(F32), 0.0)
    head_rows = lax.broadcasted_iota(jnp.int32, (SUBLANES, CONV_CH), 0)
    acc = cw_ref[CONV_K - 1:CONV_K, :] * u_cur
    for shift in range(1, CONV_K):
        shifted = pltpu.roll(u_cur, shift, 0)
        from_prev = pltpu.roll(u_prev, shift, 0)[:SUBLANES]
        head = jnp.where(head_rows < shift, from_prev, shifted[:SUBLANES])
        shifted = jnp.concatenate([head, shifted[SUBLANES:]], axis=0)
        acc = acc + cw_ref[CONV_K - 1 - shift:CONV_K - shift, :] * shifted
    yb = gb_ref[0].astype(F32) * acc
    yb_ref[0] = (_rms_normalize(yb) * gg_b_ref[...]).astype(yb_ref.dtype)


def _swa_conv(sinks, qc, kc, vc, gb, gc, xb, conv_w, gg_b, gg_c):
    batch, seq, _ = qc.shape
    blocks = ATTN_BLOCKS
    tail_per_step = blocks * BLOCK // CONV_TAIL

    def cur(width):
        return pl.BlockSpec((1, blocks * BLOCK, width), lambda b, i: (b, i, 0))

    def prev(width):
        return pl.BlockSpec((1, BLOCK, width), lambda b, i: (b, jnp.maximum(i * blocks - 1, 0), 0))

    def tail(width):
        return pl.BlockSpec((1, CONV_TAIL, width), lambda b, i: (b, jnp.maximum(i * tail_per_step - 1, 0), 0))

    def const(shape):
        return pl.BlockSpec(shape, lambda b, i: (0,) * len(shape))

    return pl.pallas_call(
        functools.partial(_swa_conv_kernel, blocks=blocks),
        grid=(batch, seq // (blocks * BLOCK)),
        in_specs=[
            pl.BlockSpec(memory_space=pltpu.SMEM), const((2, 2 * BLOCK, 2 * BLOCK)),
            cur(C_WIDTH), prev(KV_WIDTH), cur(KV_WIDTH), prev(KV_WIDTH), cur(KV_WIDTH),
            cur(CONV_CH), tail(CONV_CH), cur(CONV_CH), tail(CONV_CH), cur(CONV_CH),
            const((CONV_K, CONV_CH)), const((1, CONV_CH)), const((1, C_WIDTH)),
        ],
        out_specs=[cur(CONV_CH), cur(C_WIDTH)],
        out_shape=[jax.ShapeDtypeStruct((batch, seq, CONV_CH), BF16),
                   jax.ShapeDtypeStruct((batch, seq, C_WIDTH), BF16)],
        compiler_params=pltpu.CompilerParams(
            dimension_semantics=("parallel", "arbitrary"), vmem_limit_bytes=VMEM_LIMIT),
        name="swa_conv",
    )(sinks, _band_bias(C_MAX_DIST), qc, kc, kc, vc, vc, gb, gc, gc, xb, xb, conv_w, gg_b, gg_c)


def _expand_heads(w):
    head_of_col = lax.broadcasted_iota(jnp.int32, (LANES, A_WIDTH), 1) // HEAD_DIM
    lane = lax.broadcasted_iota(jnp.int32, (LANES, A_WIDTH), 0)
    expand = (head_of_col == lane).astype(BF16)
    hi = w.astype(BF16)
    lo = (w - hi.astype(F32)).astype(BF16)
    return _dot(hi, expand) + _dot(lo, expand)


def _mix_mlp_kernel(x_ref, o1_ref, o2_ref, o3_ref, l1_ref, l2_ref, l3_ref, yb_ref, yc_ref,
                    gg_a_ref, wo_ref, g_mlp_ref, w1_ref, w2_ref, g_fin_ref, out_ref, *, final):
    lses = (l1_ref[...], l2_ref[...], l3_ref[...])
    top = jnp.maximum(jnp.maximum(lses[0], lses[1]), lses[2])
    es = [jnp.exp(l - top) for l in lses]
    inv = 1.0 / (es[0] + es[1] + es[2])
    ya = jnp.zeros((x_ref.shape[0], A_WIDTH), F32)
    for e, o_ref in zip(es, (o1_ref, o2_ref, o3_ref)):
        ya = ya + _expand_heads(e * inv) * o_ref[...].astype(F32)
    ya = (_rms_normalize(ya) * gg_a_ref[...]).astype(BF16)

    mixed = (_dot(ya, wo_ref[0:A_WIDTH, :])
             + _dot(yb_ref[...], wo_ref[A_WIDTH:A_WIDTH + CONV_CH, :])
             + _dot(yc_ref[...], wo_ref[A_WIDTH + CONV_CH:MIX_WIDTH, :]))
    x = x_ref[...] + mixed

    h = (_rms_normalize(x) * g_mlp_ref[...]).astype(BF16)
    ff = jnp.zeros_like(x)
    for c in range(D_FF // FF_CHUNK):
        a = jnp.square(jnp.maximum(_dot(h, w1_ref[:, c * FF_CHUNK:(c + 1) * FF_CHUNK]), 0.0))
        ff = ff + _dot(a.astype(BF16), w2_ref[c * FF_CHUNK:(c + 1) * FF_CHUNK, :])
    x = x + ff
    if final:
        x = _rms_normalize(x) * g_fin_ref[...]
    out_ref[...] = x


def _mix_mlp(x2d, o1, o2, o3, l1, l2, l3, yb, yc, gg_a, wo, g_mlp, w1, w2, g_fin, final):
    n = x2d.shape[0]
    tm = MLP_TOKENS

    def rows(width):
        return pl.BlockSpec((tm, width), lambda i: (i, 0))

    def const(shape):
        return pl.BlockSpec(shape, lambda i: (0, 0), pipeline_mode=pl.Buffered(1))

    return pl.pallas_call(
        functools.partial(_mix_mlp_kernel, final=final),
        grid=(n // tm,),
        in_specs=[
            rows(D_MODEL), rows(A_WIDTH), rows(A_WIDTH), rows(A_WIDTH),
            rows(LANES), rows(LANES), rows(LANES), rows(CONV_CH), rows(C_WIDTH),
            const((1, A_WIDTH)), const((MIX_WIDTH, D_MODEL)), const((1, D_MODEL)),
            const((D_MODEL, D_FF)), const((D_FF, D_MODEL)), const((1, D_MODEL)),
        ],
        out_specs=rows(D_MODEL),
        out_shape=jax.ShapeDtypeStruct((n, D_MODEL), F32),
        compiler_params=pltpu.CompilerParams(
            dimension_semantics=("parallel",), vmem_limit_bytes=VMEM_LIMIT),
        name="mix_mlp",
    )(x2d, o1, o2, o3, l1, l2, l3, yb, yc, gg_a, wo, g_mlp, w1, w2, g_fin)


def _to_residues(t, batch, seq, dil):
    w = t.shape[-1]
    return t.reshape(batch, seq // dil, dil, w).transpose(0, 2, 1, 3).reshape(batch * dil, seq // dil, w)


def _from_residues(t, batch, seq, dil):
    w = t.shape[-1]
    return t.reshape(batch, dil, seq // dil, w).transpose(0, 2, 1, 3).reshape(batch * seq, w)


def kernel(x, w_in, conv_w, sinks, g_mix, g_group, w_o, g_mlp, w_ff_in, w_ff_out, g_final):
    batch, seq, _ = x.shape
    depth = w_in.shape[0]
    n = batch * seq
    qc_start = sum(IN_SPLITS[:6])
    c_start = A_WIDTH + CONV_CH
    c_perm = np.concatenate([np.arange(h * HEAD_DIM, (h + 1) * HEAD_DIM) for h in C_HEAD_ORDER])
    in_cols = np.arange(IN_WIDTH)
    in_cols[qc_start:qc_start + C_WIDTH] = qc_start + c_perm
    q_scale = np.ones((IN_WIDTH,), np.float32)
    q_scale[0:A_WIDTH] = HEAD_DIM ** -0.5
    q_scale[qc_start:qc_start + C_WIDTH] = HEAD_DIM ** -0.5

    def seqs(t):
        return t.reshape(batch, seq, t.shape[-1])

    x2d = x.reshape(n, D_MODEL)
    for l in range(depth):
        w_in_l = (w_in[l][:, in_cols] * q_scale).astype(BF16)
        qa, ka, va, gb, gc, xb, qc, kc, vc = _in_proj(x2d, g_mix[l].reshape(1, D_MODEL), w_in_l)

        outs, lses = [], []
        for dil in DILATIONS:
            o, lse = _dilated_attn(_to_residues(qa, batch, seq, dil), _to_residues(ka, batch, seq, dil),
                                   _to_residues(va, batch, seq, dil))
            outs.append(_from_residues(o, batch, seq, dil))
            lses.append(_from_residues(lse, batch, seq, dil))

        gg = g_group[l]
        yb, yc = _swa_conv(sinks[l].reshape(C_Q_HEADS), seqs(qc), seqs(kc), seqs(vc),
                           seqs(gb), seqs(gc), seqs(xb), conv_w[l],
                           gg[A_WIDTH:c_start].reshape(1, CONV_CH), gg[c_start:][c_perm].reshape(1, C_WIDTH))

        w_o_l = jnp.concatenate([w_o[l][:c_start], w_o[l][c_start:][c_perm]], axis=0).astype(BF16)
        x2d = _mix_mlp(x2d, outs[0], outs[1], outs[2], lses[0], lses[1], lses[2],
                       yb.reshape(n, CONV_CH), yc.reshape(n, C_WIDTH),
                       gg[:A_WIDTH].reshape(1, A_WIDTH), w_o_l,
                       g_mlp[l].reshape(1, D_MODEL), w_ff_in[l].astype(BF16), w_ff_out[l].astype(BF16),
                       g_final.reshape(1, D_MODEL), final=(l == depth - 1))
    return x2d.reshape(batch, seq, D_MODEL)
```

```python
import functools

import jax
import jax.numpy as jnp
import numpy as np
from jax import lax
from jax.experimental import pallas as pl
from jax.experimental.pallas import tpu as pltpu

D_MODEL = 1024
HEAD_DIM = 64
A_HEADS = 6
DILATIONS = (1, 4, 16)
A_MAX_DIST = 128
CONV_CH = 256
CONV_K = 3
C_Q_HEADS = 6
C_KV_HEADS = 2
C_GROUP = C_Q_HEADS // C_KV_HEADS
C_MAX_DIST = 127
BLOCK = 128
D_FF = 4 * D_MODEL
EPS = 1e-6

A_WIDTH = A_HEADS * HEAD_DIM
C_WIDTH = C_Q_HEADS * HEAD_DIM
KV_WIDTH = C_KV_HEADS * HEAD_DIM
MIX_WIDTH = A_WIDTH + CONV_CH + C_WIDTH
IN_SPLITS = (A_WIDTH, A_WIDTH, A_WIDTH, CONV_CH, CONV_CH, CONV_CH, C_WIDTH, KV_WIDTH, KV_WIDTH)
IN_WIDTH = sum(IN_SPLITS)
LANES = 128
SUBLANES = 8
A_GROUPS = A_WIDTH // LANES
CONV_TAIL = 16
C_HEAD_ORDER = (0, 3, 1, 4, 2, 5)
ATTN_BLOCKS = 4

PROJ_TOKENS = 512
MLP_TOKENS = 256
FF_CHUNK = 1024
VMEM_LIMIT = 56 * 1024 * 1024

NEG = -1e30
BF16 = jnp.bfloat16
F32 = jnp.float32


def _rms_normalize(t):
    return t * lax.rsqrt(jnp.mean(t * t, axis=-1, keepdims=True) + EPS)


def _dot(a, b):
    return jnp.dot(a, b, preferred_element_type=F32)


def _dot_nt(a, b):
    return lax.dot_general(a, b, (((1,), (1,)), ((), ())), preferred_element_type=F32)


def _lane_group(g):
    return slice(g * LANES, (g + 1) * LANES)


def _in_proj_kernel(x_ref, g_ref, w_ref, *refs):
    natural = refs[0:3]
    strided = (refs[3:6], refs[6:9])
    others = refs[9:15]
    stage = refs[15]
    tm = x_ref.shape[1]
    h = (_rms_normalize(x_ref[0]) * g_ref[...]).astype(BF16)

    def project(t):
        z = _dot(h, w_ref[:, t * A_WIDTH:(t + 1) * A_WIDTH])
        natural[t][0] = z.astype(BF16)
        for g in range(A_GROUPS):
            stage[t * A_GROUPS + g] = z[:, _lane_group(g)]

    def write_subsequences(t):
        for dil, out_refs in zip(DILATIONS[1:], strided):
            for r in range(dil):
                for g in range(A_GROUPS):
                    rows = stage[t * A_GROUPS + g, pl.ds(r, tm // dil, stride=dil), :]
                    out_refs[t][0, r, :, _lane_group(g)] = rows.astype(BF16)

    project(0)
    project(1)
    write_subsequences(0)
    project(2)
    write_subsequences(1)
    start = 3 * A_WIDTH
    for width, o_ref in zip(IN_SPLITS[3:], others):
        o_ref[0] = _dot(h, w_ref[:, start:start + width]).astype(BF16)
        start += width
    write_subsequences(2)


def _in_proj(x, g, w):
    batch, seq, _ = x.shape
    tm = PROJ_TOKENS

    def rows(width):
        return pl.BlockSpec((1, tm, width), lambda b, i: (b, i, 0))

    def sub(dil):
        return pl.BlockSpec((1, dil, tm // dil, A_WIDTH), lambda b, i: (b, 0, i, 0))

    def const(shape):
        return pl.BlockSpec(shape, lambda b, i: (0, 0), pipeline_mode=pl.Buffered(1))

    out_specs = [rows(A_WIDTH)] * 3
    out_shape = [jax.ShapeDtypeStruct((batch, seq, A_WIDTH), BF16)] * 3
    for dil in DILATIONS[1:]:
        out_specs += [sub(dil)] * 3
        out_shape += [jax.ShapeDtypeStruct((batch, dil, seq // dil, A_WIDTH), BF16)] * 3
    out_specs += [rows(wd) for wd in IN_SPLITS[3:]]
    out_shape += [jax.ShapeDtypeStruct((batch, seq, wd), BF16) for wd in IN_SPLITS[3:]]
    return pl.pallas_call(
        _in_proj_kernel,
        grid=(batch, seq // tm),
        in_specs=[rows(D_MODEL), const((1, D_MODEL)), const((D_MODEL, IN_WIDTH))],
        out_specs=out_specs,
        out_shape=out_shape,
        scratch_shapes=[pltpu.VMEM((3 * A_GROUPS, tm, LANES), F32)],
        compiler_params=pltpu.CompilerParams(
            dimension_semantics=("parallel", "parallel"), vmem_limit_bytes=VMEM_LIMIT),
        name="in_proj",
    )(x, g, w)


def _band_bias(max_dist):
    qi = np.arange(2 * BLOCK)[:, None] % BLOCK
    kj = np.arange(2 * BLOCK)[None, :]
    dist = BLOCK + qi - kj
    band = (dist >= 0) & (dist <= max_dist)
    first = band & (kj >= BLOCK)
    return jnp.asarray(np.where(np.stack([first, band]), 0.0, NEG), F32)


def _low_half():
    return lax.broadcasted_iota(jnp.int32, (BLOCK, LANES), 1) < HEAD_DIM


def _paired_attention(qs, ks, vs, bias, sinks=None):
    low = _low_half()
    ones = jnp.ones((2 * BLOCK, LANES), BF16)
    scores = []
    for q2, k2 in zip(qs, ks):
        zero = jnp.zeros_like(q2)
        stacked = jnp.concatenate([jnp.where(low, q2, zero), jnp.where(low, zero, q2)], axis=0)
        scores.append(_dot_nt(stacked, k2) + bias)
    probs, tops = [], []
    for g, s in enumerate(scores):
        m = jnp.max(s, axis=-1, keepdims=True)
        if sinks is not None:
            m = jnp.maximum(m, sinks[g])
        probs.append(jnp.exp(s - m).astype(BF16))
        tops.append(m)
    raw = [_dot(p, jnp.concatenate([v2, ones], axis=1)) for p, v2 in zip(probs, vs)]
    outs, denoms = [], []
    for g, r in enumerate(raw):
        denom = r[:, LANES:]
        if sinks is not None:
            denom = denom + jnp.exp(sinks[g] - tops[g])
        o = r[:, :LANES] / denom
        outs.append(jnp.where(low, o[:BLOCK], o[BLOCK:]))
        denoms.append(denom)
    return outs, tops, denoms


def _block_rows(jb):
    return slice(jb * BLOCK, (jb + 1) * BLOCK)


def _keys_of_block(prev_ref, cur_ref, jb, cols):
    before = prev_ref[0, :, cols] if jb == 0 else cur_ref[0, _block_rows(jb - 1), cols]
    return jnp.concatenate([before, cur_ref[0, _block_rows(jb), cols]], axis=0)


def _dilated_attn_kernel(bias_ref, q_ref, kp_ref, kc_ref, vp_ref, vc_ref, o_ref, lse_ref, *, blocks):
    lane = lax.broadcasted_iota(jnp.int32, (BLOCK, LANES), 1)
    groups = [_lane_group(g) for g in range(A_GROUPS)]
    for jb in range(blocks):
        bias = bias_ref[jnp.minimum(pl.program_id(1), 1)] if jb == 0 else bias_ref[1]
        rows = _block_rows(jb)
        outs, tops, denoms = _paired_attention(
            [q_ref[0, rows, cols] for cols in groups],
            [_keys_of_block(kp_ref, kc_ref, jb, cols) for cols in groups],
            [_keys_of_block(vp_ref, vc_ref, jb, cols) for cols in groups], bias)
        lse_tile = jnp.zeros((BLOCK, LANES), F32)
        for g, cols in enumerate(groups):
            o_ref[0, rows, cols] = outs[g].astype(o_ref.dtype)
            lse = tops[g] + jnp.log(denoms[g])
            lse_tile = jnp.where(lane == 2 * g, lse[:BLOCK], lse_tile)
            lse_tile = jnp.where(lane == 2 * g + 1, lse[BLOCK:], lse_tile)
        lse_ref[0, rows, :] = lse_tile


def _dilated_attn(q, k, v):
    n_seq, length, _ = q.shape
    blocks = min(ATTN_BLOCKS, length // BLOCK)
    cur = pl.BlockSpec((1, blocks * BLOCK, A_WIDTH), lambda s, i: (s, i, 0))
    prev = pl.BlockSpec((1, BLOCK, A_WIDTH), lambda s, i: (s, jnp.maximum(i * blocks - 1, 0), 0))
    return pl.pallas_call(
        functools.partial(_dilated_attn_kernel, blocks=blocks),
        grid=(n_seq, length // (blocks * BLOCK)),
        in_specs=[pl.BlockSpec((2, 2 * BLOCK, 2 * BLOCK), lambda s, i: (0, 0, 0)),
                  cur, prev, cur, prev, cur],
        out_specs=[cur, pl.BlockSpec((1, blocks * BLOCK, LANES), lambda s, i: (s, i, 0))],
        out_shape=[jax.ShapeDtypeStruct((n_seq, length, A_WIDTH), BF16),
                   jax.ShapeDtypeStruct((n_seq, length, LANES), F32)],
        compiler_params=pltpu.CompilerParams(
            dimension_semantics=("parallel", "arbitrary"), vmem_limit_bytes=VMEM_LIMIT),
        name="dilated_attn",
    )(_band_bias(A_MAX_DIST), q, k, k, v, v)


def _swa_conv_kernel(sink_ref, bias_ref, q_ref, kp_ref, kc_ref, vp_ref, vc_ref,
                     gb_ref, gcp_ref, gcc_ref, xbp_ref, xbc_ref, cw_ref, gg_b_ref, gg_c_ref,
                     yb_ref, yc_ref, *, blocks):
    not_first = pl.program_id(1) > 0
    groups = [_lane_group(g) for g in range(C_WIDTH // LANES)]
    upper = lax.broadcasted_iota(jnp.int32, (2 * BLOCK, 1), 0) < BLOCK
    sinks = [jnp.where(upper, sink_ref[C_HEAD_ORDER[2 * g]], sink_ref[C_HEAD_ORDER[2 * g + 1]])
             for g in range(len(groups))]
    everything = slice(0, KV_WIDTH)
    for jb in range(blocks):
        bias = bias_ref[jnp.minimum(pl.program_id(1), 1)] if jb == 0 else bias_ref[1]
        rows = _block_rows(jb)
        k2 = _keys_of_block(kp_ref, kc_ref, jb, everything)
        v2 = _keys_of_block(vp_ref, vc_ref, jb, everything)
        outs, _, _ = _paired_attention([q_ref[0, rows, cols] for cols in groups],
                                       [k2] * len(groups), [v2] * len(groups), bias, sinks)
        oc = jnp.concatenate(outs, axis=1)
        yc_ref[0, rows, :] = (_rms_normalize(oc) * gg_c_ref[...]).astype(yc_ref.dtype)

    u_cur = gcc_ref[0].astype(F32) * xbc_ref[0].astype(F32)
    u_prev = jnp.where(not_first, gcp_ref[0].astype(F32) * xbp_ref[0].astype(F32), 0.0)
    head_rows = lax.broadcasted_iota(jnp.int32, (SUBLANES, CONV_CH), 0)
    acc = cw_ref[CONV_K - 1:CONV_K, :] * u_cur
    for shift in range(1, CONV_K):
        shifted = pltpu.roll(u_cur, shift, 0)
        from_prev = pltpu.roll(u_prev, shift, 0)[:SUBLANES]
        head = jnp.where(head_rows < shift, from_prev, shifted[:SUBLANES])
        shifted = jnp.concatenate([head, shifted[SUBLANES:]], axis=0)
        acc = acc + cw_ref[CONV_K - 1 - shift:CONV_K - shift, :] * shifted
    yb = gb_ref[0].astype(F32) * acc
    yb_ref[0] = (_rms_normalize(yb) * gg_b_ref[...]).astype(yb_ref.dtype)


def _swa_conv(sinks, qc, kc, vc, gb, gc, xb, conv_w, gg_b, gg_c):
    batch, seq, _ = qc.shape
    blocks = ATTN_BLOCKS
    tail_per_step = blocks * BLOCK // CONV_TAIL

    def cur(width):
        return pl.BlockSpec((1, blocks * BLOCK, width), lambda b, i: (b, i, 0))

    def prev(width):
        return pl.BlockSpec((1, BLOCK, width), lambda b, i: (b, jnp.maximum(i * blocks - 1, 0), 0))

    def tail(width):
        return pl.BlockSpec((1, CONV_TAIL, width), lambda b, i: (b, jnp.maximum(i * tail_per_step - 1, 0), 0))

    def const(shape):
        return pl.BlockSpec(shape, lambda b, i: (0,) * len(shape))

    return pl.pallas_call(
        functools.partial(_swa_conv_kernel, blocks=blocks),
        grid=(batch, seq // (blocks * BLOCK)),
        in_specs=[
            pl.BlockSpec(memory_space=pltpu.SMEM), const((2, 2 * BLOCK, 2 * BLOCK)),
            cur(C_WIDTH), prev(KV_WIDTH), cur(KV_WIDTH), prev(KV_WIDTH), cur(KV_WIDTH),
            cur(CONV_CH), tail(CONV_CH), cur(CONV_CH), tail(CONV_CH), cur(CONV_CH),
            const((CONV_K, CONV_CH)), const((1, CONV_CH)), const((1, C_WIDTH)),
        ],
        out_specs=[cur(CONV_CH), cur(C_WIDTH)],
        out_shape=[jax.ShapeDtypeStruct((batch, seq, CONV_CH), BF16),
                   jax.ShapeDtypeStruct((batch, seq, C_WIDTH), BF16)],
        compiler_params=pltpu.CompilerParams(
            dimension_semantics=("parallel", "arbitrary"), vmem_limit_bytes=VMEM_LIMIT),
        name="swa_conv",
    )(sinks, _band_bias(C_MAX_DIST), qc, kc, kc, vc, vc, gb, gc, gc, xb, xb, conv_w, gg_b, gg_c)


def _expand_heads(w):
    head_of_col = lax.broadcasted_iota(jnp.int32, (LANES, A_WIDTH), 1) // HEAD_DIM
    lane = lax.broadcasted_iota(jnp.int32, (LANES, A_WIDTH), 0)
    expand = (head_of_col == lane).astype(BF16)
    hi = w.astype(BF16)
    lo = (w - hi.astype(F32)).astype(BF16)
    return _dot(hi, expand) + _dot(lo, expand)


def _mix_mlp_kernel(x_ref, o1_ref, o2_ref, o3_ref, l1_ref, l2_ref, l3_ref, yb_ref, yc_ref,
                    gg_a_ref, wo_ref, g_mlp_ref, w1_ref, w2_ref, g_fin_ref, out_ref, o_nat, l_nat, *, final):
    tm = x_ref.shape[1]
    for j, (dil, o_ref, l_ref) in enumerate(zip(DILATIONS[1:], (o2_ref, o3_ref), (l2_ref, l3_ref))):
        for r in range(dil):
            rows = pl.ds(r, tm // dil, stride=dil)
            l_nat[j, rows, :] = l_ref[0, r]
            for g in range(A_GROUPS):
                o_nat[j * A_GROUPS + g, rows, :] = o_ref[0, r, :, _lane_group(g)].astype(F32)
    outs = [o1_ref[0].astype(F32)] + [
        jnp.concatenate([o_nat[j * A_GROUPS + g] for g in range(A_GROUPS)], axis=1) for j in range(2)]
    lses = (l1_ref[0], l_nat[0], l_nat[1])
    top = jnp.maximum(jnp.maximum(lses[0], lses[1]), lses[2])
    es = [jnp.exp(l - top) for l in lses]
    inv = 1.0 / (es[0] + es[1] + es[2])
    ya = jnp.zeros((tm, A_WIDTH), F32)
    for e, o in zip(es, outs):
        ya = ya + _expand_heads(e * inv) * o
    ya = (_rms_normalize(ya) * gg_a_ref[...]).astype(BF16)

    mixed = (_dot(ya, wo_ref[0:A_WIDTH, :])
             + _dot(yb_ref[0], wo_ref[A_WIDTH:A_WIDTH + CONV_CH, :])
             + _dot(yc_ref[0], wo_ref[A_WIDTH + CONV_CH:MIX_WIDTH, :]))
    x = x_ref[0] + mixed

    h = (_rms_normalize(x) * g_mlp_ref[...]).astype(BF16)
    ff = jnp.zeros_like(x)
    for c in range(D_FF // FF_CHUNK):
        a = jnp.square(jnp.maximum(_dot(h, w1_ref[:, c * FF_CHUNK:(c + 1) * FF_CHUNK]), 0.0))
        ff = ff + _dot(a.astype(BF16), w2_ref[c * FF_CHUNK:(c + 1) * FF_CHUNK, :])
    x = x + ff
    if final:
        x = _rms_normalize(x) * g_fin_ref[...]
    out_ref[0] = x


def _mix_mlp(x, outs, lses, yb, yc, gg_a, wo, g_mlp, w1, w2, g_fin, final):
    batch, seq, _ = x.shape
    tm = MLP_TOKENS

    def rows(width):
        return pl.BlockSpec((1, tm, width), lambda b, i: (b, i, 0))

    def sub(dil, width):
        return pl.BlockSpec((1, dil, tm // dil, width), lambda b, i: (b, 0, i, 0))

    def const(shape):
        return pl.BlockSpec(shape, lambda b, i: (0, 0), pipeline_mode=pl.Buffered(1))

    return pl.pallas_call(
        functools.partial(_mix_mlp_kernel, final=final),
        grid=(batch, seq // tm),
        in_specs=[
            rows(D_MODEL), rows(A_WIDTH), sub(DILATIONS[1], A_WIDTH), sub(DILATIONS[2], A_WIDTH),
            rows(LANES), sub(DILATIONS[1], LANES), sub(DILATIONS[2], LANES), rows(CONV_CH), rows(C_WIDTH),
            const((1, A_WIDTH)), const((MIX_WIDTH, D_MODEL)), const((1, D_MODEL)),
            const((D_MODEL, D_FF)), const((D_FF, D_MODEL)), const((1, D_MODEL)),
        ],
        out_specs=rows(D_MODEL),
        out_shape=jax.ShapeDtypeStruct((batch, seq, D_MODEL), F32),
        scratch_shapes=[pltpu.VMEM((2 * A_GROUPS, tm, LANES), F32), pltpu.VMEM((2, tm, LANES), F32)],
        compiler_params=pltpu.CompilerParams(
            dimension_semantics=("parallel", "parallel"), vmem_limit_bytes=VMEM_LIMIT),
        name="mix_mlp",
    )(x, outs[0], outs[1], outs[2], lses[0], lses[1], lses[2], yb, yc, gg_a, wo, g_mlp, w1, w2, g_fin)


def kernel(x, w_in, conv_w, sinks, g_mix, g_group, w_o, g_mlp, w_ff_in, w_ff_out, g_final):
    batch, seq, _ = x.shape
    depth = w_in.shape[0]
    qc_start = sum(IN_SPLITS[:6])
    c_start = A_WIDTH + CONV_CH
    q_scale = np.ones((IN_WIDTH,), np.float32)
    q_scale[0:A_WIDTH] = HEAD_DIM ** -0.5
    q_scale[qc_start:qc_start + C_WIDTH] = HEAD_DIM ** -0.5

    def reorder_c_heads(t, axis):
        heads = [lax.slice_in_dim(t, h * HEAD_DIM, (h + 1) * HEAD_DIM, axis=axis) for h in C_HEAD_ORDER]
        return jnp.concatenate(heads, axis=axis)

    for l in range(depth):
        w = w_in[l] * q_scale
        w = jnp.concatenate([w[:, :qc_start], reorder_c_heads(w[:, qc_start:qc_start + C_WIDTH], 1),
                             w[:, qc_start + C_WIDTH:]], axis=1).astype(BF16)
        (qa, ka, va, q4, k4, v4, q16, k16, v16, gb, gc, xb, qc, kc, vc) = _in_proj(
            x, g_mix[l].reshape(1, D_MODEL), w)

        outs, lses = [], []
        for dil, (q, k, v) in zip(DILATIONS, ((qa, ka, va), (q4, k4, v4), (q16, k16, v16))):
            o, lse = _dilated_attn(*(t.reshape(batch * dil, seq // dil, A_WIDTH) for t in (q, k, v)))
            shape = (batch, seq) if dil == 1 else (batch, dil, seq // dil)
            outs.append(o.reshape(shape + (A_WIDTH,)))
            lses.append(lse.reshape(shape + (LANES,)))

        gg = g_group[l]
        yb, yc = _swa_conv(sinks[l].reshape(C_Q_HEADS), qc, kc, vc, gb, gc, xb, conv_w[l],
                           gg[A_WIDTH:c_start].reshape(1, CONV_CH),
                           reorder_c_heads(gg[c_start:], 0).reshape(1, C_WIDTH))

        w_o_l = jnp.concatenate([w_o[l][:c_start], reorder_c_heads(w_o[l][c_start:], 0)], axis=0).astype(BF16)
        x = _mix_mlp(x, outs, lses, yb, yc, gg[:A_WIDTH].reshape(1, A_WIDTH), w_o_l,
                     g_mlp[l].reshape(1, D_MODEL), w_ff_in[l].astype(BF16), w_ff_out[l].astype(BF16),
                     g_final.reshape(1, D_MODEL), final=(l == depth - 1))
    return x
```

```python
import functools

import jax
import jax.numpy as jnp
import numpy as np
from jax import lax
from jax.experimental import pallas as pl
from jax.experimental.pallas import tpu as pltpu

D_MODEL = 1024
HEAD_DIM = 64
A_HEADS = 6
DILATIONS = (1, 4, 16)
A_MAX_DIST = 128
CONV_CH = 256
CONV_K = 3
C_Q_HEADS = 6
C_KV_HEADS = 2
C_GROUP = C_Q_HEADS // C_KV_HEADS
C_MAX_DIST = 127
BLOCK = 128
D_FF = 4 * D_MODEL
EPS = 1e-6

A_WIDTH = A_HEADS * HEAD_DIM
C_WIDTH = C_Q_HEADS * HEAD_DIM
KV_WIDTH = C_KV_HEADS * HEAD_DIM
MIX_WIDTH = A_WIDTH + CONV_CH + C_WIDTH
IN_SPLITS = (A_WIDTH, A_WIDTH, A_WIDTH, CONV_CH, CONV_CH, CONV_CH, C_WIDTH, KV_WIDTH, KV_WIDTH)
IN_WIDTH = sum(IN_SPLITS)
LANES = 128
SUBLANES = 8
A_GROUPS = A_WIDTH // LANES
CONV_TAIL = 16
C_HEAD_ORDER = (0, 3, 1, 4, 2, 5)
ATTN_BLOCKS = 8
LOG2E = 1.4426950408889634

PROJ_TOKENS = 512
PROJ_PART = 256
PROJ_CHUNK = 512
MLP_TOKENS = 512
MLP_PART = 256
FF_CHUNK = 1024
VMEM_LIMIT = 56 * 1024 * 1024

NEG = -1e30
BF16 = jnp.bfloat16
F32 = jnp.float32


def _rms_normalize(t):
    return t * lax.rsqrt(jnp.mean(t * t, axis=-1, keepdims=True) + EPS)


def _dot(a, b):
    return jnp.dot(a, b, preferred_element_type=F32)


def _dot_nt(a, b):
    return lax.dot_general(a, b, (((1,), (1,)), ((), ())), preferred_element_type=F32)


def _lane_group(g):
    return slice(g * LANES, (g + 1) * LANES)


def _in_proj_kernel(x_ref, g_ref, w_ref, *refs):
    natural = refs[0:3]
    strided = (refs[3:6], refs[6:9])
    others = refs[9:15]
    h_ref = refs[15]
    stage = refs[16:16 + 3 * A_GROUPS]
    stage4 = refs[16 + 3 * A_GROUPS:]
    tm = x_ref.shape[1]
    parts = [slice(i * PROJ_PART, (i + 1) * PROJ_PART) for i in range(tm // PROJ_PART)]

    for part in parts:
        h_ref[part, :] = (_rms_normalize(x_ref[0, part, :]) * g_ref[...]).astype(BF16)

    dest = []
    for o_ref, width in zip(natural + others, IN_SPLITS):
        dest += [(o_ref, off) for off in range(0, width, LANES)]

    def write_subsequences(t):
        quarter = tm // 4
        for g in range(A_GROUPS):
            buf, buf4 = stage[t * A_GROUPS + g], stage4[t * A_GROUPS + g]
            for b in range(4):
                rows = buf[pl.ds(b, quarter, stride=4), :]
                strided[0][t][0, b, :, _lane_group(g)] = rows.astype(BF16)
                buf4[b * quarter:(b + 1) * quarter, :] = rows
            for a in range(4):
                for b in range(4):
                    rows = buf4[pl.ds(b * quarter + a, quarter // 4, stride=4), :]
                    strided[1][t][0, 4 * a + b, :, _lane_group(g)] = rows.astype(BF16)

    staged = 0
    for c in range(IN_WIDTH // PROJ_CHUNK):
        for part in parts:
            z = _dot(h_ref[part, :], w_ref[:, c * PROJ_CHUNK:(c + 1) * PROJ_CHUNK])
            for k in range(PROJ_CHUNK // LANES):
                j = c * (PROJ_CHUNK // LANES) + k
                o_ref, off = dest[j]
                o_ref[0, part, off:off + LANES] = z[:, _lane_group(k)].astype(BF16)
                if j < 3 * A_GROUPS:
                    stage[j][part, :] = z[:, _lane_group(k)]
        while staged < 3 and (staged + 1) * A_GROUPS <= (c + 1) * (PROJ_CHUNK // LANES):
            write_subsequences(staged)
            staged += 1


def _in_proj(x, g, w):
    batch, seq, _ = x.shape
    tm = PROJ_TOKENS

    def rows(width):
        return pl.BlockSpec((1, tm, width), lambda b, i: (b, i, 0))

    def sub(dil):
        return pl.BlockSpec((1, dil, tm // dil, A_WIDTH), lambda b, i: (b, 0, i, 0))

    def const(shape):
        return pl.BlockSpec(shape, lambda b, i: (0, 0), pipeline_mode=pl.Buffered(1))

    out_specs = [rows(A_WIDTH)] * 3
    out_shape = [jax.ShapeDtypeStruct((batch, seq, A_WIDTH), BF16)] * 3
    for dil in DILATIONS[1:]:
        out_specs += [sub(dil)] * 3
        out_shape += [jax.ShapeDtypeStruct((batch, dil, seq // dil, A_WIDTH), BF16)] * 3
    out_specs += [rows(wd) for wd in IN_SPLITS[3:]]
    out_shape += [jax.ShapeDtypeStruct((batch, seq, wd), BF16) for wd in IN_SPLITS[3:]]
    return pl.pallas_call(
        _in_proj_kernel,
        grid=(batch, seq // tm),
        in_specs=[rows(D_MODEL), const((1, D_MODEL)), const((D_MODEL, IN_WIDTH))],
        out_specs=out_specs,
        out_shape=out_shape,
        scratch_shapes=[pltpu.VMEM((tm, D_MODEL), BF16)] + [pltpu.VMEM((tm, LANES), F32)] * (6 * A_GROUPS),
        compiler_params=pltpu.CompilerParams(
            dimension_semantics=("parallel", "parallel"), vmem_limit_bytes=VMEM_LIMIT),
        name="in_proj",
    )(x, g, w)


def _band_bias(max_dist):
    qi = np.arange(2 * BLOCK)[:, None] % BLOCK
    kj = np.arange(2 * BLOCK)[None, :]
    dist = BLOCK + qi - kj
    band = (dist >= 0) & (dist <= max_dist)
    first = band & (kj >= BLOCK)
    return jnp.asarray(np.where(np.stack([first, band]), 0.0, NEG), F32)


def _low_half():
    return lax.broadcasted_iota(jnp.int32, (BLOCK, LANES), 1) < HEAD_DIM


def _pair_scores(q2, k2, bias):
    low = _low_half()
    zero = jnp.zeros_like(q2)
    stacked = jnp.concatenate([jnp.where(low, q2, zero), jnp.where(low, zero, q2)], axis=0)
    return _dot_nt(stacked, k2) + bias


def _pair_probs(s, sink=None):
    m = jnp.max(s, axis=-1, keepdims=True)
    if sink is not None:
        m = jnp.maximum(m, sink)
    return jnp.exp2(s - m).astype(BF16), m


def _pair_values(p, v2):
    ones = jnp.ones((2 * BLOCK, LANES), BF16)
    return _dot(p, jnp.concatenate([v2, ones], axis=1))


def _both_heads(t):
    return jnp.where(_low_half(), t[:BLOCK], t[BLOCK:])


def _pipeline(units, scores_of, probs_of, finish):
    n = len(units)
    scores, probs = {}, {}
    for t in range(n + 2):
        if t >= 2:
            finish(units[t - 2], probs.pop(t - 2))
        if 1 <= t <= n:
            probs[t - 1] = probs_of(units[t - 1], scores.pop(t - 1))
        if t < n:
            scores[t] = scores_of(units[t])


def _block_rows(jb):
    return slice(jb * BLOCK, (jb + 1) * BLOCK)


def _keys_of_block(prev_ref, cur_ref, sq, jb, cols):
    before = prev_ref[sq, :, cols] if jb == 0 else cur_ref[sq, _block_rows(jb - 1), cols]
    return jnp.concatenate([before, cur_ref[sq, _block_rows(jb), cols]], axis=0)


def _dilated_attn_kernel(bias_ref, q_ref, kp_ref, kc_ref, vp_ref, vc_ref, o_ref, m_ref, l_ref, *, seqs, blocks):
    lane = lax.broadcasted_iota(jnp.int32, (BLOCK, LANES), 1)
    first_bias = jnp.minimum(pl.program_id(1), 1)
    units = [(sq, jb, g) for sq in range(seqs) for jb in range(blocks) for g in range(A_GROUPS)]
    tiles = {}

    def scores_of(unit):
        sq, jb, g = unit
        bias = bias_ref[first_bias] if jb == 0 else bias_ref[1]
        return _pair_scores(q_ref[sq, _block_rows(jb), _lane_group(g)],
                            _keys_of_block(kp_ref, kc_ref, sq, jb, _lane_group(g)), bias)

    def probs_of(unit, s):
        return _pair_probs(s)

    def finish(unit, probs):
        sq, jb, g = unit
        p, m = probs
        r = _pair_values(p, _keys_of_block(vp_ref, vc_ref, sq, jb, _lane_group(g)))
        o_ref[sq, _block_rows(jb), _lane_group(g)] = _both_heads(r[:, :LANES]).astype(o_ref.dtype)
        m_tile, l_tile = tiles.pop((sq, jb), (jnp.zeros((BLOCK, LANES), F32), jnp.ones((BLOCK, LANES), F32)))
        for half in range(2):
            pick = lane == 2 * g + half
            m_tile = jnp.where(pick, m[_block_rows(half)], m_tile)
            l_tile = jnp.where(pick, r[_block_rows(half), LANES:], l_tile)
        if g == A_GROUPS - 1:
            m_ref[sq, _block_rows(jb), :] = m_tile
            l_ref[sq, _block_rows(jb), :] = l_tile
        else:
            tiles[(sq, jb)] = (m_tile, l_tile)

    _pipeline(units, scores_of, probs_of, finish)


def _dilated_attn(q, k, v):
    n_seq, length, _ = q.shape
    blocks = min(ATTN_BLOCKS, length // BLOCK)
    seqs = ATTN_BLOCKS // blocks
    cur = pl.BlockSpec((seqs, blocks * BLOCK, A_WIDTH), lambda s, i: (s, i, 0))
    prev = pl.BlockSpec((seqs, BLOCK, A_WIDTH), lambda s, i: (s, jnp.maximum(i * blocks - 1, 0), 0))
    stat = pl.BlockSpec((seqs, blocks * BLOCK, LANES), lambda s, i: (s, i, 0))
    return pl.pallas_call(
        functools.partial(_dilated_attn_kernel, seqs=seqs, blocks=blocks),
        grid=(n_seq // seqs, length // (blocks * BLOCK)),
        in_specs=[pl.BlockSpec((2, 2 * BLOCK, 2 * BLOCK), lambda s, i: (0, 0, 0)),
                  cur, prev, cur, prev, cur],
        out_specs=[cur, stat, stat],
        out_shape=[jax.ShapeDtypeStruct((n_seq, length, A_WIDTH), BF16),
                   jax.ShapeDtypeStruct((n_seq, length, LANES), F32),
                   jax.ShapeDtypeStruct((n_seq, length, LANES), F32)],
        compiler_params=pltpu.CompilerParams(
            dimension_semantics=("parallel", "arbitrary"), vmem_limit_bytes=VMEM_LIMIT),
        name="dilated_attn",
    )(_band_bias(A_MAX_DIST), q, k, k, v, v)


def _swa_conv_kernel(sink_ref, bias_ref, q_ref, kp_ref, kc_ref, vp_ref, vc_ref,
                     gb_ref, gcp_ref, gcc_ref, xbp_ref, xbc_ref, cw_ref, gg_b_ref, gg_c_ref,
                     yb_ref, yc_ref, *, blocks):
    not_first = pl.program_id(1) > 0
    first_bias = jnp.minimum(pl.program_id(1), 1)
    n_groups = C_WIDTH // LANES
    upper = lax.broadcasted_iota(jnp.int32, (2 * BLOCK, 1), 0) < BLOCK
    sinks = [jnp.where(upper, sink_ref[C_HEAD_ORDER[2 * g]], sink_ref[C_HEAD_ORDER[2 * g + 1]]) * LOG2E
             for g in range(n_groups)]
    everything = slice(0, KV_WIDTH)
    units = [(jb, g) for jb in range(blocks) for g in range(n_groups)]
    parts = {}

    def scores_of(unit):
        jb, g = unit
        bias = bias_ref[first_bias] if jb == 0 else bias_ref[1]
        return _pair_scores(q_ref[0, _block_rows(jb), _lane_group(g)],
                            _keys_of_block(kp_ref, kc_ref, 0, jb, everything), bias)

    def probs_of(unit, s):
        return _pair_probs(s, sinks[unit[1]])

    def finish(unit, probs):
        jb, g = unit
        p, m = probs
        r = _pair_values(p, _keys_of_block(vp_ref, vc_ref, 0, jb, everything))
        denom = r[:, LANES:] + jnp.exp2(sinks[g] - m)
        parts.setdefault(jb, []).append(_both_heads(r[:, :LANES]) / _both_heads(denom))
        if g == n_groups - 1:
            oc = jnp.concatenate(parts.pop(jb), axis=1)
            yc_ref[0, _block_rows(jb), :] = (_rms_normalize(oc) * gg_c_ref[...]).astype(yc_ref.dtype)

    _pipeline(units, scores_of, probs_of, finish)

    u_cur = gcc_ref[0].astype(F32) * xbc_ref[0].astype(F32)
    u_prev = jnp.where(not_first, gcp_ref[0].astype(F32) * xbp_ref[0].astype(F32), 0.0)
    head_rows = lax.broadcasted_iota(jnp.int32, (SUBLANES, CONV_CH), 0)
    acc = cw_ref[CONV_K - 1:CONV_K, :] * u_cur
    for shift in range(1, CONV_K):
        shifted = pltpu.roll(u_cur, shift, 0)
        from_prev = pltpu.roll(u_prev, shift, 0)[:SUBLANES]
        head = jnp.where(head_rows < shift, from_prev, shifted[:SUBLANES])
        shifted = jnp.concatenate([head, shifted[SUBLANES:]], axis=0)
        acc = acc + cw_ref[CONV_K - 1 - shift:CONV_K - shift, :] * shifted
    yb = gb_ref[0].astype(F32) * acc
    yb_ref[0] = (_rms_normalize(yb) * gg_b_ref[...]).astype(yb_ref.dtype)


def _swa_conv(sinks, qc, kc, vc, gb, gc, xb, conv_w, gg_b, gg_c):
    batch, seq, _ = qc.shape
    blocks = ATTN_BLOCKS
    tail_per_step = blocks * BLOCK // CONV_TAIL

    def cur(width):
        return pl.BlockSpec((1, blocks * BLOCK, width), lambda b, i: (b, i, 0))

    def prev(width):
        return pl.BlockSpec((1, BLOCK, width), lambda b, i: (b, jnp.maximum(i * blocks - 1, 0), 0))

    def tail(width):
        return pl.BlockSpec((1, CONV_TAIL, width), lambda b, i: (b, jnp.maximum(i * tail_per_step - 1, 0), 0))

    def const(shape):
        return pl.BlockSpec(shape, lambda b, i: (0,) * len(shape))

    return pl.pallas_call(
        functools.partial(_swa_conv_kernel, blocks=blocks),
        grid=(batch, seq // (blocks * BLOCK)),
        in_specs=[
            pl.BlockSpec(memory_space=pltpu.SMEM), const((2, 2 * BLOCK, 2 * BLOCK)),
            cur(C_WIDTH), prev(KV_WIDTH), cur(KV_WIDTH), prev(KV_WIDTH), cur(KV_WIDTH),
            cur(CONV_CH), tail(CONV_CH), cur(CONV_CH), tail(CONV_CH), cur(CONV_CH),
            const((CONV_K, CONV_CH)), const((1, CONV_CH)), const((1, C_WIDTH)),
        ],
        out_specs=[cur(CONV_CH), cur(C_WIDTH)],
        out_shape=[jax.ShapeDtypeStruct((batch, seq, CONV_CH), BF16),
                   jax.ShapeDtypeStruct((batch, seq, C_WIDTH), BF16)],
        compiler_params=pltpu.CompilerParams(
            dimension_semantics=("parallel", "arbitrary"), vmem_limit_bytes=VMEM_LIMIT),
        name="swa_conv",
    )(sinks, _band_bias(C_MAX_DIST), qc, kc, kc, vc, vc, gb, gc, gc, xb, xb, conv_w, gg_b, gg_c)


def _expand_heads(w):
    head_of_col = lax.broadcasted_iota(jnp.int32, (LANES, A_WIDTH), 1) // HEAD_DIM
    lane = lax.broadcasted_iota(jnp.int32, (LANES, A_WIDTH), 0)
    expand = (head_of_col == lane).astype(BF16)
    hi = w.astype(BF16)
    lo = (w - hi.astype(F32)).astype(BF16)
    return _dot(hi, expand) + _dot(lo, expand)


def _mix_mlp_kernel(x_ref, o1_ref, o2_ref, o3_ref, m1_ref, m2_ref, m3_ref, l1_ref, l2_ref, l3_ref, yb_ref, yc_ref,
                    gg_a_ref, wo_ref, g_mlp_ref, w1_ref, w2_ref, g_fin_ref, out_ref, o_nat, s_nat, tmp_nat, *, final):
    tm = x_ref.shape[1]
    quarter = tm // 4

    def to_token_order(dst, tmp, src4, src16):
        for b in range(4):
            dst[0][pl.ds(b, quarter, stride=4), :] = src4(b)
            for a in range(4):
                tmp[pl.ds(b * quarter + a, quarter // 4, stride=4), :] = src16(4 * a + b)
        for b in range(4):
            dst[1][pl.ds(b, quarter, stride=4), :] = tmp[b * quarter:(b + 1) * quarter, :]

    to_token_order((s_nat.at[0], s_nat.at[2]), tmp_nat, lambda r: m2_ref[0, r], lambda r: m3_ref[0, r])
    to_token_order((s_nat.at[1], s_nat.at[3]), tmp_nat, lambda r: l2_ref[0, r], lambda r: l3_ref[0, r])
    for g in range(A_GROUPS):
        to_token_order((o_nat.at[g], o_nat.at[A_GROUPS + g]), tmp_nat,
                       lambda r: o2_ref[0, r, :, _lane_group(g)].astype(F32),
                       lambda r: o3_ref[0, r, :, _lane_group(g)].astype(F32))

    parts = [slice(i * MLP_PART, (i + 1) * MLP_PART) for i in range(tm // MLP_PART)]

    def mixed_heads(part):
        outs = [o1_ref[0, part, :].astype(F32)] + [
            jnp.concatenate([o_nat[j * A_GROUPS + g, part, :] for g in range(A_GROUPS)], axis=1) for j in range(2)]
        tops = (m1_ref[0, part, :], s_nat[0, part, :], s_nat[2, part, :])
        sums = (l1_ref[0, part, :], s_nat[1, part, :], s_nat[3, part, :])
        top = jnp.maximum(jnp.maximum(tops[0], tops[1]), tops[2])
        es = [jnp.exp2(m - top) for m in tops]
        inv = 1.0 / (es[0] * sums[0] + es[1] * sums[1] + es[2] * sums[2])
        ya = jnp.zeros((MLP_PART, A_WIDTH), F32)
        for e, o in zip(es, outs):
            ya = ya + _expand_heads(e * inv) * o
        return (_rms_normalize(ya) * gg_a_ref[...]).astype(BF16)

    ya = [mixed_heads(part) for part in parts]
    xs = [x_ref[0, part, :] + _dot(jnp.concatenate([y, yb_ref[0, part, :], yc_ref[0, part, :]], axis=1), wo_ref[...])
          for part, y in zip(parts, ya)]
    hs = [(_rms_normalize(x) * g_mlp_ref[...]).astype(BF16) for x in xs]
    ffs = [jnp.zeros((MLP_PART, D_MODEL), F32) for _ in parts]
    for c in range(D_FF // FF_CHUNK):
        cols = slice(c * FF_CHUNK, (c + 1) * FF_CHUNK)
        for i, h in enumerate(hs):
            a = jnp.square(jnp.maximum(_dot(h, w1_ref[:, cols]), 0.0))
            ffs[i] = ffs[i] + _dot(a.astype(BF16), w2_ref[cols, :])
    for part, x, ff in zip(parts, xs, ffs):
        x = x + ff
        if final:
            x = _rms_normalize(x) * g_fin_ref[...]
        out_ref[0, part, :] = x


def _mix_mlp(x, outs, tops, sums, yb, yc, gg_a, wo, g_mlp, w1, w2, g_fin, final):
    batch, seq, _ = x.shape
    tm = MLP_TOKENS

    def rows(width):
        return pl.BlockSpec((1, tm, width), lambda b, i: (b, i, 0))

    def sub(dil, width):
        return pl.BlockSpec((1, dil, tm // dil, width), lambda b, i: (b, 0, i, 0))

    def const(shape):
        return pl.BlockSpec(shape, lambda b, i: (0, 0), pipeline_mode=pl.Buffered(1))

    return pl.pallas_call(
        functools.partial(_mix_mlp_kernel, final=final),
        grid=(batch, seq // tm),
        in_specs=[
            rows(D_MODEL), rows(A_WIDTH), sub(DILATIONS[1], A_WIDTH), sub(DILATIONS[2], A_WIDTH),
            rows(LANES), sub(DILATIONS[1], LANES), sub(DILATIONS[2], LANES),
            rows(LANES), sub(DILATIONS[1], LANES), sub(DILATIONS[2], LANES), rows(CONV_CH), rows(C_WIDTH),
            const((1, A_WIDTH)), const((MIX_WIDTH, D_MODEL)), const((1, D_MODEL)),
            const((D_MODEL, D_FF)), const((D_FF, D_MODEL)), const((1, D_MODEL)),
        ],
        out_specs=rows(D_MODEL),
        out_shape=jax.ShapeDtypeStruct((batch, seq, D_MODEL), F32),
        scratch_shapes=[pltpu.VMEM((2 * A_GROUPS, tm, LANES), F32), pltpu.VMEM((4, tm, LANES), F32),
                        pltpu.VMEM((tm, LANES), F32)],
        compiler_params=pltpu.CompilerParams(
            dimension_semantics=("parallel", "parallel"), vmem_limit_bytes=VMEM_LIMIT),
        name="mix_mlp",
    )(x, *outs, *tops, *sums, yb, yc, gg_a, wo, g_mlp, w1, w2, g_fin)


def kernel(x, w_in, conv_w, sinks, g_mix, g_group, w_o, g_mlp, w_ff_in, w_ff_out, g_final):
    batch, seq, _ = x.shape
    depth = w_in.shape[0]
    qc_start = sum(IN_SPLITS[:6])
    c_start = A_WIDTH + CONV_CH
    q_scale = np.ones((IN_WIDTH,), np.float32)
    q_scale[0:A_WIDTH] = HEAD_DIM ** -0.5 * LOG2E
    q_scale[qc_start:qc_start + C_WIDTH] = HEAD_DIM ** -0.5 * LOG2E

    def reorder_c_heads(t, axis):
        heads = [lax.slice_in_dim(t, h * HEAD_DIM, (h + 1) * HEAD_DIM, axis=axis) for h in C_HEAD_ORDER]
        return jnp.concatenate(heads, axis=axis)

    for l in range(depth):
        w = w_in[l] * q_scale
        w = jnp.concatenate([w[:, :qc_start], reorder_c_heads(w[:, qc_start:qc_start + C_WIDTH], 1),
                             w[:, qc_start + C_WIDTH:]], axis=1).astype(BF16)
        (qa, ka, va, q4, k4, v4, q16, k16, v16, gb, gc, xb, qc, kc, vc) = _in_proj(
            x, g_mix[l].reshape(1, D_MODEL), w)

        outs, tops, sums = [], [], []
        for dil, (q, k, v) in zip(DILATIONS, ((qa, ka, va), (q4, k4, v4), (q16, k16, v16))):
            o, m, den = _dilated_attn(*(t.reshape(batch * dil, seq // dil, A_WIDTH) for t in (q, k, v)))
            shape = (batch, seq) if dil == 1 else (batch, dil, seq // dil)
            outs.append(o.reshape(shape + (A_WIDTH,)))
            tops.append(m.reshape(shape + (LANES,)))
            sums.append(den.reshape(shape + (LANES,)))

        gg = g_group[l]
        yb, yc = _swa_conv(sinks[l].reshape(C_Q_HEADS), qc, kc, vc, gb, gc, xb, conv_w[l],
                           gg[A_WIDTH:c_start].reshape(1, CONV_CH),
                           reorder_c_heads(gg[c_start:], 0).reshape(1, C_WIDTH))

        w_o_l = jnp.concatenate([w_o[l][:c_start], reorder_c_heads(w_o[l][c_start:], 0)], axis=0).astype(BF16)
        x = _mix_mlp(x, outs, tops, sums, yb, yc, gg[:A_WIDTH].reshape(1, A_WIDTH), w_o_l,
                     g_mlp[l].reshape(1, D_MODEL), w_ff_in[l].astype(BF16), w_ff_out[l].astype(BF16),
                     g_final.reshape(1, D_MODEL), final=(l == depth - 1))
    return x
```

```python
import functools

import jax
import jax.numpy as jnp
import numpy as np
from jax import lax
from jax.experimental import pallas as pl
from jax.experimental.pallas import tpu as pltpu

D_MODEL = 1024
HEAD_DIM = 64
A_HEADS = 6
DILATIONS = (1, 4, 16)
A_MAX_DIST = 128
CONV_CH = 256
CONV_K = 3
C_Q_HEADS = 6
C_KV_HEADS = 2
C_GROUP = C_Q_HEADS // C_KV_HEADS
C_MAX_DIST = 127
BLOCK = 128
D_FF = 4 * D_MODEL
EPS = 1e-6

A_WIDTH = A_HEADS * HEAD_DIM
C_WIDTH = C_Q_HEADS * HEAD_DIM
KV_WIDTH = C_KV_HEADS * HEAD_DIM
MIX_WIDTH = A_WIDTH + CONV_CH + C_WIDTH
IN_NAMES = ("qa", "ka", "va", "gb", "gc", "xb", "qc", "kc", "vc")
IN_SPLITS = (A_WIDTH, A_WIDTH, A_WIDTH, CONV_CH, CONV_CH, CONV_CH, C_WIDTH, KV_WIDTH, KV_WIDTH)
IN_WIDTH = sum(IN_SPLITS)
IN_START = dict(zip(IN_NAMES, np.cumsum((0,) + IN_SPLITS[:-1]).tolist()))
LANES = 128
SUBLANES = 8
A_GROUPS = A_WIDTH // LANES
CONV_TAIL = 16
C_HEAD_ORDER = (0, 3, 1, 4, 2, 5)
ATTN_BLOCKS = 8
LOG2E = 1.4426950408889634

PROJ_TOKENS = 1024
PROJ_PART = 256
PROJ_CHUNK = 512
PROJ_GROUPS = ([(name, g) for name in ("qa", "ka", "va") for g in range(3)]
               + [("gb", 0), ("gb", 1), ("qc", 0), ("gc", 0), ("xb", 0), ("gc", 1), ("xb", 1),
                  ("qc", 1), ("qc", 2), ("kc", 0), ("vc", 0)])
PROJ_OUTS = ("gb", "u", "qc", "kc", "vc")
MLP_TOKENS = 512
MLP_PART = 256
FF_CHUNK = 1024
VMEM_LIMIT = 56 * 1024 * 1024

NEG = -1e30
BF16 = jnp.bfloat16
F32 = jnp.float32


def _rms_normalize(t):
    return t * lax.rsqrt(jnp.mean(t * t, axis=-1, keepdims=True) + EPS)


def _dot(a, b):
    return jnp.dot(a, b, preferred_element_type=F32)


def _dot_nt(a, b):
    return lax.dot_general(a, b, (((1,), (1,)), ((), ())), preferred_element_type=F32)


def _lane_group(g):
    return slice(g * LANES, (g + 1) * LANES)


def _in_proj_kernel(x_ref, g_ref, w_ref, *refs):
    natural = dict(zip(("qa", "ka", "va"), refs[0:3]))
    strided = (refs[3:6], refs[6:9])
    others = dict(zip(PROJ_OUTS, refs[9:14]))
    h_ref = refs[14]
    stage = refs[15:15 + 3 * A_GROUPS]
    stage4 = refs[15 + 3 * A_GROUPS:]
    tm = x_ref.shape[1]
    parts = [slice(i * PROJ_PART, (i + 1) * PROJ_PART) for i in range(tm // PROJ_PART)]

    for part in parts:
        h_ref[part, :] = (_rms_normalize(x_ref[0, part, :]) * g_ref[...]).astype(BF16)

    def write_subsequences(t):
        quarter = tm // 4
        for g in range(A_GROUPS):
            buf, buf4 = stage[t * A_GROUPS + g], stage4[t * A_GROUPS + g]
            for b in range(4):
                rows = buf[pl.ds(b, quarter, stride=4), :]
                strided[0][t][0, b, :, _lane_group(g)] = rows.astype(BF16)
                buf4[b * quarter:(b + 1) * quarter, :] = rows
            for a in range(4):
                for b in range(4):
                    rows = buf4[pl.ds(b * quarter + a, quarter // 4, stride=4), :]
                    strided[1][t][0, 4 * a + b, :, _lane_group(g)] = rows.astype(BF16)

    staged = 0
    per_chunk = PROJ_CHUNK // LANES
    for c in range(IN_WIDTH // PROJ_CHUNK):
        for part in parts:
            z = _dot(h_ref[part, :], w_ref[:, c * PROJ_CHUNK:(c + 1) * PROJ_CHUNK])
            gates = {}
            for k in range(per_chunk):
                j = c * per_chunk + k
                name, g = PROJ_GROUPS[j]
                zk = z[:, _lane_group(k)]
                if name == "gc":
                    gates[g] = zk
                elif name == "xb":
                    others["u"][0, part, _lane_group(g)] = (gates.pop(g) * zk).astype(BF16)
                elif name in natural:
                    natural[name][0, part, _lane_group(g)] = zk.astype(BF16)
                    stage[j][part, :] = zk
                else:
                    others[name][0, part, _lane_group(g)] = zk.astype(BF16)
        while staged < 3 and (staged + 1) * A_GROUPS <= (c + 1) * per_chunk:
            write_subsequences(staged)
            staged += 1


def _in_proj(x, g, w):
    batch, seq, _ = x.shape
    tm = PROJ_TOKENS

    def rows(width):
        return pl.BlockSpec((1, tm, width), lambda b, i: (b, i, 0))

    def sub(dil):
        return pl.BlockSpec((1, dil, tm // dil, A_WIDTH), lambda b, i: (b, 0, i, 0))

    def const(shape):
        return pl.BlockSpec(shape, lambda b, i: (0, 0), pipeline_mode=pl.Buffered(1))

    out_specs = [rows(A_WIDTH)] * 3
    out_shape = [jax.ShapeDtypeStruct((batch, seq, A_WIDTH), BF16)] * 3
    for dil in DILATIONS[1:]:
        out_specs += [sub(dil)] * 3
        out_shape += [jax.ShapeDtypeStruct((batch, dil, seq // dil, A_WIDTH), BF16)] * 3
    other_widths = (CONV_CH, CONV_CH, C_WIDTH, KV_WIDTH, KV_WIDTH)
    out_specs += [rows(wd) for wd in other_widths]
    out_shape += [jax.ShapeDtypeStruct((batch, seq, wd), BF16) for wd in other_widths]
    return pl.pallas_call(
        _in_proj_kernel,
        grid=(batch, seq // tm),
        in_specs=[rows(D_MODEL), const((1, D_MODEL)), const((D_MODEL, IN_WIDTH))],
        out_specs=out_specs,
        out_shape=out_shape,
        scratch_shapes=[pltpu.VMEM((tm, D_MODEL), BF16)] + [pltpu.VMEM((tm, LANES), F32)] * (6 * A_GROUPS),
        compiler_params=pltpu.CompilerParams(
            dimension_semantics=("parallel", "parallel"), vmem_limit_bytes=VMEM_LIMIT),
        name="in_proj",
    )(x, g, w)


def _band_bias(max_dist):
    qi = np.arange(2 * BLOCK)[:, None] % BLOCK
    kj = np.arange(2 * BLOCK)[None, :]
    dist = BLOCK + qi - kj
    band = (dist >= 0) & (dist <= max_dist)
    first = band & (kj >= BLOCK)
    return jnp.asarray(np.where(np.stack([first, band]), 0.0, NEG), F32)


def _low_half():
    return lax.broadcasted_iota(jnp.int32, (BLOCK, LANES), 1) < HEAD_DIM


def _pair_scores(q2, k2, bias):
    low = _low_half()
    zero = jnp.zeros_like(q2)
    stacked = jnp.concatenate([jnp.where(low, q2, zero), jnp.where(low, zero, q2)], axis=0)
    return _dot_nt(stacked, k2) + bias


def _pair_probs(s, sink=None):
    m = jnp.max(s, axis=-1, keepdims=True)
    if sink is not None:
        m = jnp.maximum(m, sink)
    return jnp.exp2(s - m).astype(BF16), m


def _pair_values(p, v2):
    ones = jnp.ones((2 * BLOCK, LANES), BF16)
    return _dot(p, jnp.concatenate([v2, ones], axis=1))


def _both_heads(t):
    return jnp.where(_low_half(), t[:BLOCK], t[BLOCK:])


def _pipeline(units, scores_of, probs_of, finish):
    n = len(units)
    scores, probs = {}, {}
    for t in range(n + 2):
        if t >= 2:
            finish(units[t - 2], probs.pop(t - 2))
        if 1 <= t <= n:
            probs[t - 1] = probs_of(units[t - 1], scores.pop(t - 1))
        if t < n:
            scores[t] = scores_of(units[t])


def _block_rows(jb):
    return slice(jb * BLOCK, (jb + 1) * BLOCK)


def _keys_of_block(prev_ref, cur_ref, sq, jb, cols):
    before = prev_ref[sq, :, cols] if jb == 0 else cur_ref[sq, _block_rows(jb - 1), cols]
    return jnp.concatenate([before, cur_ref[sq, _block_rows(jb), cols]], axis=0)


def _dilated_attn_kernel(bias_ref, q_ref, kp_ref, kc_ref, vp_ref, vc_ref, o_ref, m_ref, l_ref, *, seqs, blocks):
    lane = lax.broadcasted_iota(jnp.int32, (BLOCK, LANES), 1)
    first_bias = jnp.minimum(pl.program_id(1), 1)
    units = [(sq, jb, g) for sq in range(seqs) for jb in range(blocks) for g in range(A_GROUPS)]
    tiles = {}

    def scores_of(unit):
        sq, jb, g = unit
        bias = bias_ref[first_bias] if jb == 0 else bias_ref[1]
        return _pair_scores(q_ref[sq, _block_rows(jb), _lane_group(g)],
                            _keys_of_block(kp_ref, kc_ref, sq, jb, _lane_group(g)), bias)

    def probs_of(unit, s):
        return _pair_probs(s)

    def finish(unit, probs):
        sq, jb, g = unit
        p, m = probs
        r = _pair_values(p, _keys_of_block(vp_ref, vc_ref, sq, jb, _lane_group(g)))
        o_ref[sq, _block_rows(jb), _lane_group(g)] = _both_heads(r[:, :LANES]).astype(o_ref.dtype)
        m_tile, l_tile = tiles.pop((sq, jb), (jnp.zeros((BLOCK, LANES), F32), jnp.ones((BLOCK, LANES), F32)))
        for half in range(2):
            pick = lane == 2 * g + half
            m_tile = jnp.where(pick, m[_block_rows(half)], m_tile)
            l_tile = jnp.where(pick, r[_block_rows(half), LANES:], l_tile)
        if g == A_GROUPS - 1:
            m_ref[sq, _block_rows(jb), :] = m_tile
            l_ref[sq, _block_rows(jb), :] = l_tile
        else:
            tiles[(sq, jb)] = (m_tile, l_tile)

    _pipeline(units, scores_of, probs_of, finish)


def _dilated_attn(q, k, v):
    n_seq, length, _ = q.shape
    blocks = min(ATTN_BLOCKS, length // BLOCK)
    seqs = ATTN_BLOCKS // blocks
    cur = pl.BlockSpec((seqs, blocks * BLOCK, A_WIDTH), lambda s, i: (s, i, 0))
    prev = pl.BlockSpec((seqs, BLOCK, A_WIDTH), lambda s, i: (s, jnp.maximum(i * blocks - 1, 0), 0))
    stat = pl.BlockSpec((seqs, blocks * BLOCK, LANES), lambda s, i: (s, i, 0))
    return pl.pallas_call(
        functools.partial(_dilated_attn_kernel, seqs=seqs, blocks=blocks),
        grid=(n_seq // seqs, length // (blocks * BLOCK)),
        in_specs=[pl.BlockSpec((2, 2 * BLOCK, 2 * BLOCK), lambda s, i: (0, 0, 0)),
                  cur, prev, cur, prev, cur],
        out_specs=[cur, stat, stat],
        out_shape=[jax.ShapeDtypeStruct((n_seq, length, A_WIDTH), BF16),
                   jax.ShapeDtypeStruct((n_seq, length, LANES), F32),
                   jax.ShapeDtypeStruct((n_seq, length, LANES), F32)],
        compiler_params=pltpu.CompilerParams(
            dimension_semantics=("parallel", "arbitrary"), vmem_limit_bytes=VMEM_LIMIT),
        name="dilated_attn",
    )(_band_bias(A_MAX_DIST), q, k, k, v, v)


def _swa_kernel(sink_ref, bias_ref, q_ref, kp_ref, kc_ref, vp_ref, vc_ref, gg_c_ref, yc_ref, *, blocks):
    first_bias = jnp.minimum(pl.program_id(1), 1)
    n_groups = C_WIDTH // LANES
    upper = lax.broadcasted_iota(jnp.int32, (2 * BLOCK, 1), 0) < BLOCK
    sinks = [jnp.where(upper, sink_ref[C_HEAD_ORDER[2 * g]], sink_ref[C_HEAD_ORDER[2 * g + 1]]) * LOG2E
             for g in range(n_groups)]
    everything = slice(0, KV_WIDTH)
    units = [(jb, g) for jb in range(blocks) for g in range(n_groups)]
    parts = {}

    def scores_of(unit):
        jb, g = unit
        bias = bias_ref[first_bias] if jb == 0 else bias_ref[1]
        return _pair_scores(q_ref[0, _block_rows(jb), _lane_group(g)],
                            _keys_of_block(kp_ref, kc_ref, 0, jb, everything), bias)

    def probs_of(unit, s):
        return _pair_probs(s, sinks[unit[1]])

    def finish(unit, probs):
        jb, g = unit
        p, m = probs
        r = _pair_values(p, _keys_of_block(vp_ref, vc_ref, 0, jb, everything))
        denom = r[:, LANES:] + jnp.exp2(sinks[g] - m)
        parts.setdefault(jb, []).append(_both_heads(r[:, :LANES]) / _both_heads(denom))
        if g == n_groups - 1:
            oc = jnp.concatenate(parts.pop(jb), axis=1)
            yc_ref[0, _block_rows(jb), :] = (_rms_normalize(oc) * gg_c_ref[...]).astype(yc_ref.dtype)

    _pipeline(units, scores_of, probs_of, finish)


def _swa(sinks, qc, kc, vc, gg_c):
    batch, seq, _ = qc.shape
    blocks = ATTN_BLOCKS

    def cur(width):
        return pl.BlockSpec((1, blocks * BLOCK, width), lambda b, i: (b, i, 0))

    def prev(width):
        return pl.BlockSpec((1, BLOCK, width), lambda b, i: (b, jnp.maximum(i * blocks - 1, 0), 0))

    def const(shape):
        return pl.BlockSpec(shape, lambda b, i: (0,) * len(shape))

    return pl.pallas_call(
        functools.partial(_swa_kernel, blocks=blocks),
        grid=(batch, seq // (blocks * BLOCK)),
        in_specs=[
            pl.BlockSpec(memory_space=pltpu.SMEM), const((2, 2 * BLOCK, 2 * BLOCK)),
            cur(C_WIDTH), prev(KV_WIDTH), cur(KV_WIDTH), prev(KV_WIDTH), cur(KV_WIDTH), const((1, C_WIDTH)),
        ],
        out_specs=cur(C_WIDTH),
        out_shape=jax.ShapeDtypeStruct((batch, seq, C_WIDTH), BF16),
        compiler_params=pltpu.CompilerParams(
            dimension_semantics=("parallel", "arbitrary"), vmem_limit_bytes=VMEM_LIMIT),
        name="swa",
    )(sinks, _band_bias(C_MAX_DIST), qc, kc, kc, vc, vc, gg_c)


def _expand_heads(w):
    head_of_col = lax.broadcasted_iota(jnp.int32, (LANES, A_WIDTH), 1) // HEAD_DIM
    lane = lax.broadcasted_iota(jnp.int32, (LANES, A_WIDTH), 0)
    expand = (head_of_col == lane).astype(BF16)
    hi = w.astype(BF16)
    lo = (w - hi.astype(F32)).astype(BF16)
    return _dot(hi, expand) + _dot(lo, expand)


def _gated_conv(gb_ref, u_ref, u_tail_ref, cw_ref, part):
    u_cur = u_ref[0, part, :].astype(F32)
    if part.start == 0:
        u_prev = jnp.where(pl.program_id(1) > 0, u_tail_ref[0].astype(F32), 0.0)
    else:
        u_prev = u_ref[0, part.start - CONV_TAIL:part.start, :].astype(F32)
    head_rows = lax.broadcasted_iota(jnp.int32, (SUBLANES, CONV_CH), 0)
    acc = cw_ref[CONV_K - 1:CONV_K, :] * u_cur
    for shift in range(1, CONV_K):
        shifted = pltpu.roll(u_cur, shift, 0)
        from_prev = pltpu.roll(u_prev, shift, 0)[:SUBLANES]
        head = jnp.where(head_rows < shift, from_prev, shifted[:SUBLANES])
        shifted = jnp.concatenate([head, shifted[SUBLANES:]], axis=0)
        acc = acc + cw_ref[CONV_K - 1 - shift:CONV_K - shift, :] * shifted
    return gb_ref[0, part, :].astype(F32) * acc


def _mix_mlp_kernel(x_ref, o1_ref, o2_ref, o3_ref, m1_ref, m2_ref, m3_ref, l1_ref, l2_ref, l3_ref,
                    gb_ref, u_ref, u_tail_ref, yc_ref, cw_ref, gg_a_ref, gg_b_ref, wo_ref, g_mlp_ref, w1_ref, w2_ref,
                    g_fin_ref, out_ref, o_nat, s_nat, tmp_nat, *, final):
    tm = x_ref.shape[1]
    quarter = tm // 4

    def to_token_order(dst, tmp, src4, src16):
        for b in range(4):
            dst[0][pl.ds(b, quarter, stride=4), :] = src4(b)
            for a in range(4):
                tmp[pl.ds(b * quarter + a, quarter // 4, stride=4), :] = src16(4 * a + b)
        for b in range(4):
            dst[1][pl.ds(b, quarter, stride=4), :] = tmp[b * quarter:(b + 1) * quarter, :]

    to_token_order((s_nat.at[0], s_nat.at[2]), tmp_nat, lambda r: m2_ref[0, r], lambda r: m3_ref[0, r])
    to_token_order((s_nat.at[1], s_nat.at[3]), tmp_nat, lambda r: l2_ref[0, r], lambda r: l3_ref[0, r])
    for g in range(A_GROUPS):
        to_token_order((o_nat.at[g], o_nat.at[A_GROUPS + g]), tmp_nat,
                       lambda r: o2_ref[0, r, :, _lane_group(g)].astype(F32),
                       lambda r: o3_ref[0, r, :, _lane_group(g)].astype(F32))

    parts = [slice(i * MLP_PART, (i + 1) * MLP_PART) for i in range(tm // MLP_PART)]

    def mixed_heads(part):
        outs = [o1_ref[0, part, :].astype(F32)] + [
            jnp.concatenate([o_nat[j * A_GROUPS + g, part, :] for g in range(A_GROUPS)], axis=1) for j in range(2)]
        tops = (m1_ref[0, part, :], s_nat[0, part, :], s_nat[2, part, :])
        sums = (l1_ref[0, part, :], s_nat[1, part, :], s_nat[3, part, :])
        top = jnp.maximum(jnp.maximum(tops[0], tops[1]), tops[2])
        es = [jnp.exp2(m - top) for m in tops]
        inv = 1.0 / (es[0] * sums[0] + es[1] * sums[1] + es[2] * sums[2])
        ya = jnp.zeros((MLP_PART, A_WIDTH), F32)
        for e, o in zip(es, outs):
            ya = ya + _expand_heads(e * inv) * o
        return (_rms_normalize(ya) * gg_a_ref[...]).astype(BF16)

    ya = [mixed_heads(part) for part in parts]
    yb = [(_rms_normalize(_gated_conv(gb_ref, u_ref, u_tail_ref, cw_ref, part)) * gg_b_ref[...]).astype(BF16)
          for part in parts]
    xs = [x_ref[0, part, :] + _dot(jnp.concatenate([a, b, yc_ref[0, part, :]], axis=1), wo_ref[...])
          for part, a, b in zip(parts, ya, yb)]
    hs = [(_rms_normalize(x) * g_mlp_ref[...]).astype(BF16) for x in xs]
    ffs = [jnp.zeros((MLP_PART, D_MODEL), F32) for _ in parts]
    for c in range(D_FF // FF_CHUNK):
        cols = slice(c * FF_CHUNK, (c + 1) * FF_CHUNK)
        for i, h in enumerate(hs):
            a = jnp.square(jnp.maximum(_dot(h, w1_ref[:, cols]), 0.0))
            ffs[i] = ffs[i] + _dot(a.astype(BF16), w2_ref[cols, :])
    for part, x, ff in zip(parts, xs, ffs):
        x = x + ff
        if final:
            x = _rms_normalize(x) * g_fin_ref[...]
        out_ref[0, part, :] = x


def _mix_mlp(x, outs, tops, sums, gb, u, yc, conv_w, gg_a, gg_b, wo, g_mlp, w1, w2, g_fin, final):
    batch, seq, _ = x.shape
    tm = MLP_TOKENS

    def rows(width):
        return pl.BlockSpec((1, tm, width), lambda b, i: (b, i, 0))

    def sub(dil, width):
        return pl.BlockSpec((1, dil, tm // dil, width), lambda b, i: (b, 0, i, 0))

    def const(shape):
        return pl.BlockSpec(shape, lambda b, i: (0, 0), pipeline_mode=pl.Buffered(1))

    tail = pl.BlockSpec((1, CONV_TAIL, CONV_CH), lambda b, i: (b, jnp.maximum(i * (tm // CONV_TAIL) - 1, 0), 0))
    return pl.pallas_call(
        functools.partial(_mix_mlp_kernel, final=final),
        grid=(batch, seq // tm),
        in_specs=[
            rows(D_MODEL), rows(A_WIDTH), sub(DILATIONS[1], A_WIDTH), sub(DILATIONS[2], A_WIDTH),
            rows(LANES), sub(DILATIONS[1], LANES), sub(DILATIONS[2], LANES),
            rows(LANES), sub(DILATIONS[1], LANES), sub(DILATIONS[2], LANES),
            rows(CONV_CH), rows(CONV_CH), tail, rows(C_WIDTH),
            const((CONV_K, CONV_CH)), const((1, A_WIDTH)), const((1, CONV_CH)), const((MIX_WIDTH, D_MODEL)),
            const((1, D_MODEL)),
            const((D_MODEL, D_FF)), const((D_FF, D_MODEL)), const((1, D_MODEL)),
        ],
        out_specs=rows(D_MODEL),
        out_shape=jax.ShapeDtypeStruct((batch, seq, D_MODEL), F32),
        scratch_shapes=[pltpu.VMEM((2 * A_GROUPS, tm, LANES), F32), pltpu.VMEM((4, tm, LANES), F32),
                        pltpu.VMEM((tm, LANES), F32)],
        compiler_params=pltpu.CompilerParams(
            dimension_semantics=("parallel", "parallel"), vmem_limit_bytes=VMEM_LIMIT),
        name="mix_mlp",
    )(x, *outs, *tops, *sums, gb, u, u, yc, conv_w, gg_a, gg_b, wo, g_mlp, w1, w2, g_fin)


def kernel(x, w_in, conv_w, sinks, g_mix, g_group, w_o, g_mlp, w_ff_in, w_ff_out, g_final):
    batch, seq, _ = x.shape
    depth = w_in.shape[0]
    qc_start = sum(IN_SPLITS[:6])
    c_start = A_WIDTH + CONV_CH
    q_scale = np.ones((IN_WIDTH,), np.float32)
    q_scale[0:A_WIDTH] = HEAD_DIM ** -0.5 * LOG2E
    q_scale[qc_start:qc_start + C_WIDTH] = HEAD_DIM ** -0.5 * LOG2E

    def reorder_c_heads(t, axis):
        heads = [lax.slice_in_dim(t, h * HEAD_DIM, (h + 1) * HEAD_DIM, axis=axis) for h in C_HEAD_ORDER]
        return jnp.concatenate(heads, axis=axis)

    def project_columns(w):
        w = w * q_scale
        w = jnp.concatenate([w[:, :qc_start], reorder_c_heads(w[:, qc_start:qc_start + C_WIDTH], 1),
                             w[:, qc_start + C_WIDTH:]], axis=1)
        groups = [lax.slice_in_dim(w, IN_START[name] + g * LANES, IN_START[name] + (g + 1) * LANES, axis=1)
                  for name, g in PROJ_GROUPS]
        return jnp.concatenate(groups, axis=1).astype(BF16)

    for l in range(depth):
        (qa, ka, va, q4, k4, v4, q16, k16, v16, gb, u, qc, kc, vc) = _in_proj(
            x, g_mix[l].reshape(1, D_MODEL), project_columns(w_in[l]))

        outs, tops, sums = [], [], []
        for dil, (q, k, v) in zip(DILATIONS, ((qa, ka, va), (q4, k4, v4), (q16, k16, v16))):
            o, m, den = _dilated_attn(*(t.reshape(batch * dil, seq // dil, A_WIDTH) for t in (q, k, v)))
            shape = (batch, seq) if dil == 1 else (batch, dil, seq // dil)
            outs.append(o.reshape(shape + (A_WIDTH,)))
            tops.append(m.reshape(shape + (LANES,)))
            sums.append(den.reshape(shape + (LANES,)))

        gg = g_group[l]
        yc = _swa(sinks[l].reshape(C_Q_HEADS), qc, kc, vc, reorder_c_heads(gg[c_start:], 0).reshape(1, C_WIDTH))

        w_o_l = jnp.concatenate([w_o[l][:c_start], reorder_c_heads(w_o[l][c_start:], 0)], axis=0).astype(BF16)
        x = _mix_mlp(x, outs, tops, sums, gb, u, yc, conv_w[l], gg[:A_WIDTH].reshape(1, A_WIDTH),
                     gg[A_WIDTH:c_start].reshape(1, CONV_CH), w_o_l,
                     g_mlp[l].reshape(1, D_MODEL), w_ff_in[l].astype(BF16), w_ff_out[l].astype(BF16),
                     g_final.reshape(1, D_MODEL), final=(l == depth - 1))
    return x
```

```python
import functools

import jax
import jax.numpy as jnp
import numpy as np
from jax import lax
from jax.experimental import pallas as pl
from jax.experimental.pallas import tpu as pltpu

D_MODEL = 1024
HEAD_DIM = 64
A_HEADS = 6
DILATIONS = (1, 4, 16)
A_MAX_DIST = 128
CONV_CH = 256
CONV_K = 3
C_Q_HEADS = 6
C_KV_HEADS = 2
C_GROUP = C_Q_HEADS // C_KV_HEADS
C_MAX_DIST = 127
BLOCK = 128
D_FF = 4 * D_MODEL
EPS = 1e-6

A_WIDTH = A_HEADS * HEAD_DIM
C_WIDTH = C_Q_HEADS * HEAD_DIM
KV_WIDTH = C_KV_HEADS * HEAD_DIM
MIX_WIDTH = A_WIDTH + CONV_CH + C_WIDTH
IN_NAMES = ("qa", "ka", "va", "gb", "gc", "xb", "qc", "kc", "vc")
IN_SPLITS = (A_WIDTH, A_WIDTH, A_WIDTH, CONV_CH, CONV_CH, CONV_CH, C_WIDTH, KV_WIDTH, KV_WIDTH)
IN_WIDTH = sum(IN_SPLITS)
IN_START = dict(zip(IN_NAMES, np.cumsum((0,) + IN_SPLITS[:-1]).tolist()))
LANES = 128
SUBLANES = 8
A_GROUPS = A_WIDTH // LANES
CONV_TAIL = 16
C_HEAD_ORDER = (0, 3, 1, 4, 2, 5)
ATTN_BLOCKS = 16
LOG2E = 1.4426950408889634
L_LANE = 64

PROJ_TOKENS = 1024
PROJ_PART = 256
PROJ_CHUNK = 512
PROJ_GROUPS = ([(name, g) for name in ("qa", "ka", "va") for g in range(3)]
               + [("gb", 0), ("gb", 1), ("qc", 0), ("gc", 0), ("xb", 0), ("gc", 1), ("xb", 1),
                  ("qc", 1), ("qc", 2), ("kc", 0), ("vc", 0)])
PROJ_OUTS = ("gb", "u", "qc", "kc", "vc")
MLP_TOKENS = 512
MLP_PART = 256
FF_CHUNK = 1024
VMEM_LIMIT = 56 * 1024 * 1024

NEG = -1e30
BF16 = jnp.bfloat16
F32 = jnp.float32


def _rms_normalize(t):
    return t * lax.rsqrt(jnp.mean(t * t, axis=-1, keepdims=True) + EPS)


def _dot(a, b):
    return jnp.dot(a, b, preferred_element_type=F32)


def _dot_nt(a, b):
    return lax.dot_general(a, b, (((1,), (1,)), ((), ())), preferred_element_type=F32)


def _lane_group(g):
    return slice(g * LANES, (g + 1) * LANES)


def _in_proj_kernel(x_ref, g_ref, w_ref, *refs, normalized):
    natural = dict(zip(("qa", "ka", "va"), refs[0:3]))
    strided = (refs[3:6], refs[6:9])
    others = dict(zip(PROJ_OUTS, refs[9:14]))
    h_ref = refs[14]
    stage = refs[15:15 + 3 * A_GROUPS]
    stage4 = refs[15 + 3 * A_GROUPS:]
    tm = x_ref.shape[1]
    parts = [slice(i * PROJ_PART, (i + 1) * PROJ_PART) for i in range(tm // PROJ_PART)]

    if not normalized:
        for part in parts:
            h_ref[part, :] = (_rms_normalize(x_ref[0, part, :]) * g_ref[...]).astype(BF16)

    def lhs(part):
        return x_ref[0, part, :] if normalized else h_ref[part, :]

    def write_subsequences(t):
        quarter = tm // 4
        for g in range(A_GROUPS):
            buf, buf4 = stage[t * A_GROUPS + g], stage4[t * A_GROUPS + g]
            for b in range(4):
                rows = buf[pl.ds(b, quarter, stride=4), :]
                strided[0][t][0, b, :, _lane_group(g)] = rows.astype(BF16)
                buf4[b * quarter:(b + 1) * quarter, :] = rows
            for a in range(4):
                for b in range(4):
                    rows = buf4[pl.ds(b * quarter + a, quarter // 4, stride=4), :]
                    strided[1][t][0, 4 * a + b, :, _lane_group(g)] = rows.astype(BF16)

    staged = 0
    per_chunk = PROJ_CHUNK // LANES
    for c in range(IN_WIDTH // PROJ_CHUNK):
        for part in parts:
            z = _dot(lhs(part), w_ref[:, c * PROJ_CHUNK:(c + 1) * PROJ_CHUNK])
            gates = {}
            for k in range(per_chunk):
                j = c * per_chunk + k
                name, g = PROJ_GROUPS[j]
                zk = z[:, _lane_group(k)]
                if name == "gc":
                    gates[g] = zk
                elif name == "xb":
                    others["u"][0, part, _lane_group(g)] = (gates.pop(g) * zk).astype(BF16)
                elif name in natural:
                    natural[name][0, part, _lane_group(g)] = zk.astype(BF16)
                    stage[j][part, :] = zk
                else:
                    others[name][0, part, _lane_group(g)] = zk.astype(BF16)
        while staged < 3 and (staged + 1) * A_GROUPS <= (c + 1) * per_chunk:
            write_subsequences(staged)
            staged += 1


def _in_proj(x, g, w, normalized):
    batch, seq, _ = x.shape
    tm = PROJ_TOKENS

    def rows(width):
        return pl.BlockSpec((1, tm, width), lambda b, i: (b, i, 0))

    def sub(dil):
        return pl.BlockSpec((1, dil, tm // dil, A_WIDTH), lambda b, i: (b, 0, i, 0))

    def const(shape):
        return pl.BlockSpec(shape, lambda b, i: (0, 0), pipeline_mode=pl.Buffered(1))

    out_specs = [rows(A_WIDTH)] * 3
    out_shape = [jax.ShapeDtypeStruct((batch, seq, A_WIDTH), BF16)] * 3
    for dil in DILATIONS[1:]:
        out_specs += [sub(dil)] * 3
        out_shape += [jax.ShapeDtypeStruct((batch, dil, seq // dil, A_WIDTH), BF16)] * 3
    other_widths = (CONV_CH, CONV_CH, C_WIDTH, KV_WIDTH, KV_WIDTH)
    out_specs += [rows(wd) for wd in other_widths]
    out_shape += [jax.ShapeDtypeStruct((batch, seq, wd), BF16) for wd in other_widths]
    return pl.pallas_call(
        functools.partial(_in_proj_kernel, normalized=normalized),
        grid=(batch, seq // tm),
        in_specs=[rows(D_MODEL), const((1, D_MODEL)), const((D_MODEL, IN_WIDTH))],
        out_specs=out_specs,
        out_shape=out_shape,
        scratch_shapes=[pltpu.VMEM((tm, D_MODEL), BF16)] + [pltpu.VMEM((tm, LANES), F32)] * (6 * A_GROUPS),
        compiler_params=pltpu.CompilerParams(
            dimension_semantics=("parallel", "parallel"), vmem_limit_bytes=VMEM_LIMIT),
        name="in_proj",
    )(x, g, w)


def _band_bias(max_dist):
    qi = np.arange(2 * BLOCK)[:, None] % BLOCK
    kj = np.arange(2 * BLOCK)[None, :]
    dist = BLOCK + qi - kj
    band = (dist >= 0) & (dist <= max_dist)
    first = band & (kj >= BLOCK)
    return jnp.asarray(np.where(np.stack([first, band]), 0.0, NEG), F32)


def _low_half():
    return lax.broadcasted_iota(jnp.int32, (BLOCK, LANES), 1) < HEAD_DIM


def _pair_scores(q2, k2, bias):
    low = _low_half()
    zero = jnp.zeros_like(q2)
    stacked = jnp.concatenate([jnp.where(low, q2, zero), jnp.where(low, zero, q2)], axis=0)
    return _dot_nt(stacked, k2) + bias


def _pair_probs(s, sink=None):
    m = jnp.max(s, axis=-1, keepdims=True)
    if sink is not None:
        m = jnp.maximum(m, sink)
    return jnp.exp2(s - m).astype(BF16), m


def _pair_values(p, v2):
    ones = jnp.ones((2 * BLOCK, LANES), BF16)
    return _dot(p, jnp.concatenate([v2, ones], axis=1))


def _both_heads(t):
    return jnp.where(_low_half(), t[:BLOCK], t[BLOCK:])


def _pipeline(units, scores_of, probs_of, finish):
    n = len(units)
    scores, probs = {}, {}
    for t in range(n + 2):
        if t >= 2:
            finish(units[t - 2], probs.pop(t - 2))
        if 1 <= t <= n:
            probs[t - 1] = probs_of(units[t - 1], scores.pop(t - 1))
        if t < n:
            scores[t] = scores_of(units[t])


def _block_rows(jb):
    return slice(jb * BLOCK, (jb + 1) * BLOCK)


def _keys_of_block(prev_ref, cur_ref, sq, jb, cols):
    before = prev_ref[sq, :, cols] if jb == 0 else cur_ref[sq, _block_rows(jb - 1), cols]
    return jnp.concatenate([before, cur_ref[sq, _block_rows(jb), cols]], axis=0)


def _dilated_attn_kernel(bias_ref, q_ref, kp_ref, kc_ref, vp_ref, vc_ref, o_ref, st_ref, *, seqs, blocks):
    lane = lax.broadcasted_iota(jnp.int32, (BLOCK, LANES), 1)
    first_bias = jnp.minimum(pl.program_id(1), 1)
    units = [(sq, jb, g) for sq in range(seqs) for jb in range(blocks) for g in range(A_GROUPS)]
    tiles = {}

    def scores_of(unit):
        sq, jb, g = unit
        bias = bias_ref[first_bias] if jb == 0 else bias_ref[1]
        return _pair_scores(q_ref[sq, _block_rows(jb), _lane_group(g)],
                            _keys_of_block(kp_ref, kc_ref, sq, jb, _lane_group(g)), bias)

    def probs_of(unit, s):
        return _pair_probs(s)

    def finish(unit, probs):
        sq, jb, g = unit
        p, m = probs
        r = _pair_values(p, _keys_of_block(vp_ref, vc_ref, sq, jb, _lane_group(g)))
        o_ref[sq, _block_rows(jb), _lane_group(g)] = _both_heads(r[:, :LANES]).astype(o_ref.dtype)
        tile = tiles.pop((sq, jb), None)
        if tile is None:
            tile = jnp.where(lane >= L_LANE, 1.0, 0.0)
        for half in range(2):
            tile = jnp.where(lane == 2 * g + half, m[_block_rows(half)], tile)
            tile = jnp.where(lane == L_LANE + 2 * g + half, r[_block_rows(half), LANES:], tile)
        if g == A_GROUPS - 1:
            st_ref[sq, _block_rows(jb), :] = tile
        else:
            tiles[(sq, jb)] = tile

    _pipeline(units, scores_of, probs_of, finish)


def _dilated_attn(q, k, v):
    n_seq, length, _ = q.shape
    blocks = min(ATTN_BLOCKS, length // BLOCK)
    seqs = ATTN_BLOCKS // blocks
    cur = pl.BlockSpec((seqs, blocks * BLOCK, A_WIDTH), lambda s, i: (s, i, 0))
    prev = pl.BlockSpec((seqs, BLOCK, A_WIDTH), lambda s, i: (s, jnp.maximum(i * blocks - 1, 0), 0))
    stat = pl.BlockSpec((seqs, blocks * BLOCK, LANES), lambda s, i: (s, i, 0))
    return pl.pallas_call(
        functools.partial(_dilated_attn_kernel, seqs=seqs, blocks=blocks),
        grid=(n_seq // seqs, length // (blocks * BLOCK)),
        in_specs=[pl.BlockSpec((2, 2 * BLOCK, 2 * BLOCK), lambda s, i: (0, 0, 0)),
                  cur, prev, cur, prev, cur],
        out_specs=[cur, stat],
        out_shape=[jax.ShapeDtypeStruct((n_seq, length, A_WIDTH), BF16),
                   jax.ShapeDtypeStruct((n_seq, length, LANES), F32)],
        compiler_params=pltpu.CompilerParams(
            dimension_semantics=("parallel", "arbitrary"), vmem_limit_bytes=VMEM_LIMIT),
        name="dilated_attn",
    )(_band_bias(A_MAX_DIST), q, k, k, v, v)


def _swa_kernel(sink_ref, bias_ref, q_ref, kp_ref, kc_ref, vp_ref, vc_ref, gg_c_ref, yc_ref, *, blocks):
    first_bias = jnp.minimum(pl.program_id(1), 1)
    n_groups = C_WIDTH // LANES
    upper = lax.broadcasted_iota(jnp.int32, (2 * BLOCK, 1), 0) < BLOCK
    sinks = [jnp.where(upper, sink_ref[C_HEAD_ORDER[2 * g]], sink_ref[C_HEAD_ORDER[2 * g + 1]]) * LOG2E
             for g in range(n_groups)]
    everything = slice(0, KV_WIDTH)
    units = [(jb, g) for jb in range(blocks) for g in range(n_groups)]
    parts = {}

    def scores_of(unit):
        jb, g = unit
        bias = bias_ref[first_bias] if jb == 0 else bias_ref[1]
        return _pair_scores(q_ref[0, _block_rows(jb), _lane_group(g)],
                            _keys_of_block(kp_ref, kc_ref, 0, jb, everything), bias)

    def probs_of(unit, s):
        return _pair_probs(s, sinks[unit[1]])

    def finish(unit, probs):
        jb, g = unit
        p, m = probs
        r = _pair_values(p, _keys_of_block(vp_ref, vc_ref, 0, jb, everything))
        denom = r[:, LANES:] + jnp.exp2(sinks[g] - m)
        parts.setdefault(jb, []).append(_both_heads(r[:, :LANES]) / _both_heads(denom))
        if g == n_groups - 1:
            oc = jnp.concatenate(parts.pop(jb), axis=1)
            yc_ref[0, _block_rows(jb), :] = (_rms_normalize(oc) * gg_c_ref[...]).astype(yc_ref.dtype)

    _pipeline(units, scores_of, probs_of, finish)


def _swa(sinks, qc, kc, vc, gg_c):
    batch, seq, _ = qc.shape
    blocks = ATTN_BLOCKS

    def cur(width):
        return pl.BlockSpec((1, blocks * BLOCK, width), lambda b, i: (b, i, 0))

    def prev(width):
        return pl.BlockSpec((1, BLOCK, width), lambda b, i: (b, jnp.maximum(i * blocks - 1, 0), 0))

    def const(shape):
        return pl.BlockSpec(shape, lambda b, i: (0,) * len(shape))

    return pl.pallas_call(
        functools.partial(_swa_kernel, blocks=blocks),
        grid=(batch, seq // (blocks * BLOCK)),
        in_specs=[
            pl.BlockSpec(memory_space=pltpu.SMEM), const((2, 2 * BLOCK, 2 * BLOCK)),
            cur(C_WIDTH), prev(KV_WIDTH), cur(KV_WIDTH), prev(KV_WIDTH), cur(KV_WIDTH), const((1, C_WIDTH)),
        ],
        out_specs=cur(C_WIDTH),
        out_shape=jax.ShapeDtypeStruct((batch, seq, C_WIDTH), BF16),
        compiler_params=pltpu.CompilerParams(
            dimension_semantics=("parallel", "arbitrary"), vmem_limit_bytes=VMEM_LIMIT),
        name="swa",
    )(sinks, _band_bias(C_MAX_DIST), qc, kc, kc, vc, vc, gg_c)


def _expand_heads(w):
    head_of_col = lax.broadcasted_iota(jnp.int32, (LANES, A_WIDTH), 1) // HEAD_DIM
    lane = lax.broadcasted_iota(jnp.int32, (LANES, A_WIDTH), 0)
    expand = (head_of_col == lane).astype(BF16)
    hi = w.astype(BF16)
    lo = (w - hi.astype(F32)).astype(BF16)
    return _dot(hi, expand) + _dot(lo, expand)


def _gated_conv(gb_ref, u_ref, u_tail_ref, cw_ref, part):
    u_cur = u_ref[0, part, :].astype(F32)
    if part.start == 0:
        u_prev = jnp.where(pl.program_id(1) > 0, u_tail_ref[0].astype(F32), 0.0)
    else:
        u_prev = u_ref[0, part.start - CONV_TAIL:part.start, :].astype(F32)
    head_rows = lax.broadcasted_iota(jnp.int32, (SUBLANES, CONV_CH), 0)
    acc = cw_ref[CONV_K - 1:CONV_K, :] * u_cur
    for shift in range(1, CONV_K):
        shifted = pltpu.roll(u_cur, shift, 0)
        from_prev = pltpu.roll(u_prev, shift, 0)[:SUBLANES]
        head = jnp.where(head_rows < shift, from_prev, shifted[:SUBLANES])
        shifted = jnp.concatenate([head, shifted[SUBLANES:]], axis=0)
        acc = acc + cw_ref[CONV_K - 1 - shift:CONV_K - shift, :] * shifted
    return gb_ref[0, part, :].astype(F32) * acc


def _mix_mlp_kernel(x_ref, o1_ref, o2_ref, o3_ref, s1_ref, s2_ref, s3_ref,
                    gb_ref, u_ref, u_tail_ref, yc_ref, cw_ref, gg_a_ref, gg_b_ref, wo_ref, g_mlp_ref, w1_ref, w2_ref,
                    g_next_ref, *refs, final):
    if final:
        out_ref, o_nat, s_nat, tmp_nat = refs
    else:
        out_ref, h_ref, o_nat, s_nat, tmp_nat = refs
    tm = x_ref.shape[1]
    quarter = tm // 4

    def to_token_order(dst, tmp, src4, src16):
        for b in range(4):
            dst[0][pl.ds(b, quarter, stride=4), :] = src4(b)
            for a in range(4):
                tmp[pl.ds(b * quarter + a, quarter // 4, stride=4), :] = src16(4 * a + b)
        for b in range(4):
            dst[1][pl.ds(b, quarter, stride=4), :] = tmp[b * quarter:(b + 1) * quarter, :]

    to_token_order((s_nat.at[0], s_nat.at[1]), tmp_nat, lambda r: s2_ref[0, r], lambda r: s3_ref[0, r])
    for g in range(A_GROUPS):
        to_token_order((o_nat.at[g], o_nat.at[A_GROUPS + g]), tmp_nat,
                       lambda r: o2_ref[0, r, :, _lane_group(g)].astype(F32),
                       lambda r: o3_ref[0, r, :, _lane_group(g)].astype(F32))

    parts = [slice(i * MLP_PART, (i + 1) * MLP_PART) for i in range(tm // MLP_PART)]

    def mixed_heads(part):
        outs = [o1_ref[0, part, :].astype(F32)] + [
            jnp.concatenate([o_nat[j * A_GROUPS + g, part, :] for g in range(A_GROUPS)], axis=1) for j in range(2)]
        tops = (s1_ref[0, part, :], s_nat[0, part, :], s_nat[1, part, :])
        sums = [pltpu.roll(t, LANES - L_LANE, 1) for t in tops]
        top = jnp.maximum(jnp.maximum(tops[0], tops[1]), tops[2])
        es = [jnp.exp2(m - top) for m in tops]
        inv = 1.0 / (es[0] * sums[0] + es[1] * sums[1] + es[2] * sums[2])
        is_head = lax.broadcasted_iota(jnp.int32, (MLP_PART, LANES), 1) < A_HEADS
        ya = jnp.zeros((MLP_PART, A_WIDTH), F32)
        for e, o in zip(es, outs):
            ya = ya + _expand_heads(jnp.where(is_head, e * inv, 0.0)) * o
        return (_rms_normalize(ya) * gg_a_ref[...]).astype(BF16)

    ya = [mixed_heads(part) for part in parts]
    yb = [(_rms_normalize(_gated_conv(gb_ref, u_ref, u_tail_ref, cw_ref, part)) * gg_b_ref[...]).astype(BF16)
          for part in parts]
    xs = [x_ref[0, part, :] + _dot(jnp.concatenate([a, b, yc_ref[0, part, :]], axis=1), wo_ref[...])
          for part, a, b in zip(parts, ya, yb)]
    hs = [(_rms_normalize(x) * g_mlp_ref[...]).astype(BF16) for x in xs]
    ffs = [jnp.zeros((MLP_PART, D_MODEL), F32) for _ in parts]
    for c in range(D_FF // FF_CHUNK):
        cols = slice(c * FF_CHUNK, (c + 1) * FF_CHUNK)
        for i, h in enumerate(hs):
            a = jnp.square(jnp.maximum(_dot(h, w1_ref[:, cols]), 0.0))
            ffs[i] = ffs[i] + _dot(a.astype(BF16), w2_ref[cols, :])
    for part, x, ff in zip(parts, xs, ffs):
        x = x + ff
        if final:
            out_ref[0, part, :] = _rms_normalize(x) * g_next_ref[...]
        else:
            out_ref[0, part, :] = x
            h_ref[0, part, :] = (_rms_normalize(x) * g_next_ref[...]).astype(BF16)


def _mix_mlp(x, outs, stats, gb, u, yc, conv_w, gg_a, gg_b, wo, g_mlp, w1, w2, g_next, final):
    batch, seq, _ = x.shape
    tm = MLP_TOKENS

    def rows(width):
        return pl.BlockSpec((1, tm, width), lambda b, i: (b, i, 0))

    def sub(dil, width):
        return pl.BlockSpec((1, dil, tm // dil, width), lambda b, i: (b, 0, i, 0))

    def const(shape):
        return pl.BlockSpec(shape, lambda b, i: (0, 0), pipeline_mode=pl.Buffered(1))

    tail = pl.BlockSpec((1, CONV_TAIL, CONV_CH), lambda b, i: (b, jnp.maximum(i * (tm // CONV_TAIL) - 1, 0), 0))
    return pl.pallas_call(
        functools.partial(_mix_mlp_kernel, final=final),
        grid=(batch, seq // tm),
        in_specs=[
            rows(D_MODEL), rows(A_WIDTH), sub(DILATIONS[1], A_WIDTH), sub(DILATIONS[2], A_WIDTH),
            rows(LANES), sub(DILATIONS[1], LANES), sub(DILATIONS[2], LANES),
            rows(CONV_CH), rows(CONV_CH), tail, rows(C_WIDTH),
            const((CONV_K, CONV_CH)), const((1, A_WIDTH)), const((1, CONV_CH)), const((MIX_WIDTH, D_MODEL)),
            const((1, D_MODEL)),
            const((D_MODEL, D_FF)), const((D_FF, D_MODEL)), const((1, D_MODEL)),
        ],
        out_specs=[rows(D_MODEL)] * (1 if final else 2),
        out_shape=[jax.ShapeDtypeStruct((batch, seq, D_MODEL), F32)] + (
            [] if final else [jax.ShapeDtypeStruct((batch, seq, D_MODEL), BF16)]),
        scratch_shapes=[pltpu.VMEM((2 * A_GROUPS, tm, LANES), F32), pltpu.VMEM((2, tm, LANES), F32),
                        pltpu.VMEM((tm, LANES), F32)],
        compiler_params=pltpu.CompilerParams(
            dimension_semantics=("parallel", "parallel"), vmem_limit_bytes=VMEM_LIMIT),
        name="mix_mlp",
    )(x, *outs, *stats, gb, u, u, yc, conv_w, gg_a, gg_b, wo, g_mlp, w1, w2, g_next)


def kernel(x, w_in, conv_w, sinks, g_mix, g_group, w_o, g_mlp, w_ff_in, w_ff_out, g_final):
    batch, seq, _ = x.shape
    depth = w_in.shape[0]
    qc_start = sum(IN_SPLITS[:6])
    c_start = A_WIDTH + CONV_CH
    q_scale = np.ones((IN_WIDTH,), np.float32)
    q_scale[0:A_WIDTH] = HEAD_DIM ** -0.5 * LOG2E
    q_scale[qc_start:qc_start + C_WIDTH] = HEAD_DIM ** -0.5 * LOG2E

    def reorder_c_heads(t, axis):
        heads = [lax.slice_in_dim(t, h * HEAD_DIM, (h + 1) * HEAD_DIM, axis=axis) for h in C_HEAD_ORDER]
        return jnp.concatenate(heads, axis=axis)

    def project_columns(w):
        w = w * q_scale
        w = jnp.concatenate([w[:, :qc_start], reorder_c_heads(w[:, qc_start:qc_start + C_WIDTH], 1),
                             w[:, qc_start + C_WIDTH:]], axis=1)
        groups = [lax.slice_in_dim(w, IN_START[name] + g * LANES, IN_START[name] + (g + 1) * LANES, axis=1)
                  for name, g in PROJ_GROUPS]
        return jnp.concatenate(groups, axis=1).astype(BF16)

    h = None
    for l in range(depth):
        final = l == depth - 1
        (qa, ka, va, q4, k4, v4, q16, k16, v16, gb, u, qc, kc, vc) = _in_proj(
            x if h is None else h, g_mix[l].reshape(1, D_MODEL), project_columns(w_in[l]), normalized=h is not None)

        outs, stats = [], []
        for dil, (q, k, v) in zip(DILATIONS, ((qa, ka, va), (q4, k4, v4), (q16, k16, v16))):
            o, st = _dilated_attn(*(t.reshape(batch * dil, seq // dil, A_WIDTH) for t in (q, k, v)))
            shape = (batch, seq) if dil == 1 else (batch, dil, seq // dil)
            outs.append(o.reshape(shape + (A_WIDTH,)))
            stats.append(st.reshape(shape + (LANES,)))

        gg = g_group[l]
        yc = _swa(sinks[l].reshape(C_Q_HEADS), qc, kc, vc, reorder_c_heads(gg[c_start:], 0).reshape(1, C_WIDTH))

        w_o_l = jnp.concatenate([w_o[l][:c_start], reorder_c_heads(w_o[l][c_start:], 0)], axis=0).astype(BF16)
        g_next = g_final if final else g_mix[l + 1]
        res = _mix_mlp(x, outs, stats, gb, u, yc, conv_w[l], gg[:A_WIDTH].reshape(1, A_WIDTH),
                       gg[A_WIDTH:c_start].reshape(1, CONV_CH), w_o_l,
                       g_mlp[l].reshape(1, D_MODEL), w_ff_in[l].astype(BF16), w_ff_out[l].astype(BF16),
                       g_next.reshape(1, D_MODEL), final=final)
        x = res[0]
        h = None if final else res[1]
    return x
```

```python
import functools

import jax
import jax.numpy as jnp
import numpy as np
from jax import lax
from jax.experimental import pallas as pl
from jax.experimental.pallas import tpu as pltpu

D_MODEL = 1024
HEAD_DIM = 64
A_HEADS = 6
DILATIONS = (1, 4, 16)
A_MAX_DIST = 128
CONV_CH = 256
CONV_K = 3
C_Q_HEADS = 6
C_KV_HEADS = 2
C_GROUP = C_Q_HEADS // C_KV_HEADS
C_MAX_DIST = 127
BLOCK = 128
D_FF = 4 * D_MODEL
EPS = 1e-6

A_WIDTH = A_HEADS * HEAD_DIM
C_WIDTH = C_Q_HEADS * HEAD_DIM
KV_WIDTH = C_KV_HEADS * HEAD_DIM
MIX_WIDTH = A_WIDTH + CONV_CH + C_WIDTH
IN_NAMES = ("qa", "ka", "va", "gb", "gc", "xb", "qc", "kc", "vc")
IN_SPLITS = (A_WIDTH, A_WIDTH, A_WIDTH, CONV_CH, CONV_CH, CONV_CH, C_WIDTH, KV_WIDTH, KV_WIDTH)
IN_WIDTH = sum(IN_SPLITS)
IN_START = dict(zip(IN_NAMES, np.cumsum((0,) + IN_SPLITS[:-1]).tolist()))
LANES = 128
SUBLANES = 8
A_GROUPS = A_WIDTH // LANES
CONV_TAIL = 16
C_HEAD_ORDER = (0, 3, 1, 4, 2, 5)
ATTN_BLOCKS = 16
LOG2E = 1.4426950408889634
L_LANE = 64

PROJ_TOKENS = 1024
PROJ_PART = 256
PROJ_CHUNK = 512
PROJ_GROUPS = ([(name, g) for name in ("qa", "ka", "va") for g in range(3)]
               + [("gb", 0), ("gb", 1), ("qc", 0), ("gc", 0), ("xb", 0), ("gc", 1), ("xb", 1),
                  ("qc", 1), ("qc", 2), ("kc", 0), ("vc", 0)])
MLP_TOKENS = 512
MLP_PART = 256
FF_CHUNK = 1024
VMEM_LIMIT = 56 * 1024 * 1024

NEG = -1e30
BF16 = jnp.bfloat16
F32 = jnp.float32


def _rms_normalize(t):
    return t * lax.rsqrt(jnp.mean(t * t, axis=-1, keepdims=True) + EPS)


def _dot(a, b):
    return jnp.dot(a, b, preferred_element_type=F32)


def _dot_nt(a, b):
    return lax.dot_general(a, b, (((1,), (1,)), ((), ())), preferred_element_type=F32)


def _lane_group(g):
    return slice(g * LANES, (g + 1) * LANES)


LOCAL_NAMES = ("qa", "ka", "va", "qc", "kc", "vc")
LOCAL_WIDTHS = (A_WIDTH, A_WIDTH, A_WIDTH, C_WIDTH, KV_WIDTH, KV_WIDTH)
CARRY_NAMES = ("ka", "va", "kc", "vc")
CARRY_WIDTHS = (A_WIDTH, A_WIDTH, KV_WIDTH, KV_WIDTH)


def _proj_attn_kernel(sink_ref, x_ref, g_ref, w_ref, bias_a_ref, bias_c_ref, gg_c_ref, *refs, normalized):
    o1_ref, st1_ref, yc_ref = refs[0:3]
    strided = (refs[3:6], refs[6:9])
    gb_ref, u_ref = refs[9:11]
    h_ref = refs[11]
    local = dict(zip(LOCAL_NAMES, refs[12:18]))
    carry = dict(zip(CARRY_NAMES, refs[18:22]))
    stage = refs[22:22 + 3 * A_GROUPS]
    stage4 = refs[22 + 3 * A_GROUPS:]
    tm = x_ref.shape[1]
    blocks = tm // BLOCK

    @pl.when(pl.program_id(1) == 0)
    def _():
        for ref in carry.values():
            ref[...] = jnp.zeros(ref.shape, ref.dtype)

    parts = [slice(i * PROJ_PART, (i + 1) * PROJ_PART) for i in range(tm // PROJ_PART)]

    if not normalized:
        for part in parts:
            h_ref[part, :] = (_rms_normalize(x_ref[0, part, :]) * g_ref[...]).astype(BF16)

    def lhs(part):
        return x_ref[0, part, :] if normalized else h_ref[part, :]

    def write_subsequences(t):
        quarter = tm // 4
        for g in range(A_GROUPS):
            buf, buf4 = stage[t * A_GROUPS + g], stage4[t * A_GROUPS + g]
            for b in range(4):
                rows = buf[pl.ds(b, quarter, stride=4), :]
                strided[0][t][0, b, :, _lane_group(g)] = rows.astype(BF16)
                buf4[b * quarter:(b + 1) * quarter, :] = rows
            for a in range(4):
                for b in range(4):
                    rows = buf4[pl.ds(b * quarter + a, quarter // 4, stride=4), :]
                    strided[1][t][0, 4 * a + b, :, _lane_group(g)] = rows.astype(BF16)

    staged = 0
    per_chunk = PROJ_CHUNK // LANES
    for c in range(IN_WIDTH // PROJ_CHUNK):
        for part in parts:
            z = _dot(lhs(part), w_ref[:, c * PROJ_CHUNK:(c + 1) * PROJ_CHUNK])
            gates = {}
            for k in range(per_chunk):
                j = c * per_chunk + k
                name, g = PROJ_GROUPS[j]
                zk = z[:, _lane_group(k)]
                if name == "gc":
                    gates[g] = zk
                elif name == "xb":
                    u_ref[0, part, _lane_group(g)] = (gates.pop(g) * zk).astype(BF16)
                elif name == "gb":
                    gb_ref[0, part, _lane_group(g)] = zk.astype(BF16)
                else:
                    local[name][0, part, _lane_group(g)] = zk.astype(BF16)
                    if j < 3 * A_GROUPS:
                        stage[j][part, :] = zk
        while staged < 3 and (staged + 1) * A_GROUPS <= (c + 1) * per_chunk:
            write_subsequences(staged)
            staged += 1

    lane = lax.broadcasted_iota(jnp.int32, (BLOCK, LANES), 1)
    first_bias = jnp.minimum(pl.program_id(1), 1)
    upper = lax.broadcasted_iota(jnp.int32, (2 * BLOCK, 1), 0) < BLOCK
    sinks = [jnp.where(upper, sink_ref[C_HEAD_ORDER[2 * g]], sink_ref[C_HEAD_ORDER[2 * g + 1]]) * LOG2E
             for g in range(C_WIDTH // LANES)]
    kv_lanes = slice(0, KV_WIDTH)
    units = [(kind, jb, g) for jb in range(blocks) for kind in ("a", "c") for g in range(A_GROUPS)]
    tiles, c_parts = {}, {}

    def scores_of(unit):
        kind, jb, g = unit
        bias_ref = bias_a_ref if kind == "a" else bias_c_ref
        bias = bias_ref[first_bias] if jb == 0 else bias_ref[1]
        if kind == "a":
            keys = _keys_of_block(carry["ka"], local["ka"], 0, jb, _lane_group(g))
            return _pair_scores(local["qa"][0, _block_rows(jb), _lane_group(g)], keys, bias)
        keys = _keys_of_block(carry["kc"], local["kc"], 0, jb, kv_lanes)
        return _pair_scores(local["qc"][0, _block_rows(jb), _lane_group(g)], keys, bias)

    def probs_of(unit, s):
        kind, _, g = unit
        return _pair_probs(s, None if kind == "a" else sinks[g])

    def finish(unit, probs):
        kind, jb, g = unit
        p, m = probs
        if kind == "a":
            r = _pair_values(p, _keys_of_block(carry["va"], local["va"], 0, jb, _lane_group(g)))
            o1_ref[0, _block_rows(jb), _lane_group(g)] = _both_heads(r[:, :LANES]).astype(o1_ref.dtype)
            tile = tiles.pop(jb, None)
            if tile is None:
                tile = jnp.where(lane >= L_LANE, 1.0, 0.0)
            for half in range(2):
                tile = jnp.where(lane == 2 * g + half, m[_block_rows(half)], tile)
                tile = jnp.where(lane == L_LANE + 2 * g + half, r[_block_rows(half), LANES:], tile)
            if g == A_GROUPS - 1:
                st1_ref[0, _block_rows(jb), :] = tile
            else:
                tiles[jb] = tile
        else:
            r = _pair_values(p, _keys_of_block(carry["vc"], local["vc"], 0, jb, kv_lanes))
            denom = r[:, LANES:] + jnp.exp2(sinks[g] - m)
            c_parts.setdefault(jb, []).append(_both_heads(r[:, :LANES]) / _both_heads(denom))
            if g == A_GROUPS - 1:
                oc = jnp.concatenate(c_parts.pop(jb), axis=1)
                yc_ref[0, _block_rows(jb), :] = (_rms_normalize(oc) * gg_c_ref[...]).astype(yc_ref.dtype)

    _pipeline(units, scores_of, probs_of, finish)

    for name in CARRY_NAMES:
        carry[name][0] = local[name][0, _block_rows(blocks - 1), :]


def _proj_attn(x, g, w, sinks, gg_c, normalized):
    batch, seq, _ = x.shape
    tm = PROJ_TOKENS

    def rows(width):
        return pl.BlockSpec((1, tm, width), lambda b, i: (b, i, 0))

    def sub(dil):
        return pl.BlockSpec((1, dil, tm // dil, A_WIDTH), lambda b, i: (b, 0, i, 0))

    def const(shape):
        return pl.BlockSpec(shape, lambda b, i: (0,) * len(shape), pipeline_mode=pl.Buffered(1))

    out_specs = [rows(A_WIDTH), rows(LANES), rows(C_WIDTH)]
    out_shape = [jax.ShapeDtypeStruct((batch, seq, A_WIDTH), BF16), jax.ShapeDtypeStruct((batch, seq, LANES), F32),
                 jax.ShapeDtypeStruct((batch, seq, C_WIDTH), BF16)]
    for dil in DILATIONS[1:]:
        out_specs += [sub(dil)] * 3
        out_shape += [jax.ShapeDtypeStruct((batch, dil, seq // dil, A_WIDTH), BF16)] * 3
    out_specs += [rows(CONV_CH)] * 2
    out_shape += [jax.ShapeDtypeStruct((batch, seq, CONV_CH), BF16)] * 2
    scratch = ([pltpu.VMEM((tm, D_MODEL), BF16)]
               + [pltpu.VMEM((1, tm, wd), BF16) for wd in LOCAL_WIDTHS]
               + [pltpu.VMEM((1, BLOCK, wd), BF16) for wd in CARRY_WIDTHS]
               + [pltpu.VMEM((tm, LANES), F32)] * (6 * A_GROUPS))
    bias_spec = const((2, 2 * BLOCK, 2 * BLOCK))
    return pl.pallas_call(
        functools.partial(_proj_attn_kernel, normalized=normalized),
        grid=(batch, seq // tm),
        in_specs=[pl.BlockSpec(memory_space=pltpu.SMEM), rows(D_MODEL), const((1, D_MODEL)),
                  const((D_MODEL, IN_WIDTH)), bias_spec, bias_spec, const((1, C_WIDTH))],
        out_specs=out_specs,
        out_shape=out_shape,
        scratch_shapes=scratch,
        compiler_params=pltpu.CompilerParams(
            dimension_semantics=("parallel", "arbitrary"), vmem_limit_bytes=VMEM_LIMIT),
        name="proj_attn",
    )(sinks, x, g, w, _band_bias(A_MAX_DIST), _band_bias(C_MAX_DIST), gg_c)


def _band_bias(max_dist):
    qi = np.arange(2 * BLOCK)[:, None] % BLOCK
    kj = np.arange(2 * BLOCK)[None, :]
    dist = BLOCK + qi - kj
    band = (dist >= 0) & (dist <= max_dist)
    first = band & (kj >= BLOCK)
    return jnp.asarray(np.where(np.stack([first, band]), 0.0, NEG), F32)


def _low_half():
    return lax.broadcasted_iota(jnp.int32, (BLOCK, LANES), 1) < HEAD_DIM


def _pair_scores(q2, k2, bias):
    low = _low_half()
    zero = jnp.zeros_like(q2)
    stacked = jnp.concatenate([jnp.where(low, q2, zero), jnp.where(low, zero, q2)], axis=0)
    return _dot_nt(stacked, k2) + bias


def _pair_probs(s, sink=None):
    m = jnp.max(s, axis=-1, keepdims=True)
    if sink is not None:
        m = jnp.maximum(m, sink)
    return jnp.exp2(s - m).astype(BF16), m


def _pair_values(p, v2):
    ones = jnp.ones((2 * BLOCK, LANES), BF16)
    return _dot(p, jnp.concatenate([v2, ones], axis=1))


def _both_heads(t):
    return jnp.where(_low_half(), t[:BLOCK], t[BLOCK:])


def _pipeline(units, scores_of, probs_of, finish):
    n = len(units)
    scores, probs = {}, {}
    for t in range(n + 2):
        if t >= 2:
            finish(units[t - 2], probs.pop(t - 2))
        if 1 <= t <= n:
            probs[t - 1] = probs_of(units[t - 1], scores.pop(t - 1))
        if t < n:
            scores[t] = scores_of(units[t])


def _block_rows(jb):
    return slice(jb * BLOCK, (jb + 1) * BLOCK)


def _keys_of_block(prev_ref, cur_ref, sq, jb, cols):
    before = prev_ref[sq, :, cols] if jb == 0 else cur_ref[sq, _block_rows(jb - 1), cols]
    return jnp.concatenate([before, cur_ref[sq, _block_rows(jb), cols]], axis=0)


def _dilated_attn_kernel(bias_ref, q_ref, kp_ref, kc_ref, vp_ref, vc_ref, o_ref, st_ref, *, seqs, blocks):
    lane = lax.broadcasted_iota(jnp.int32, (BLOCK, LANES), 1)
    first_bias = jnp.minimum(pl.program_id(1), 1)
    units = [(sq, jb, g) for sq in range(seqs) for jb in range(blocks) for g in range(A_GROUPS)]
    tiles = {}

    def scores_of(unit):
        sq, jb, g = unit
        bias = bias_ref[first_bias] if jb == 0 else bias_ref[1]
        return _pair_scores(q_ref[sq, _block_rows(jb), _lane_group(g)],
                            _keys_of_block(kp_ref, kc_ref, sq, jb, _lane_group(g)), bias)

    def probs_of(unit, s):
        return _pair_probs(s)

    def finish(unit, probs):
        sq, jb, g = unit
        p, m = probs
        r = _pair_values(p, _keys_of_block(vp_ref, vc_ref, sq, jb, _lane_group(g)))
        o_ref[sq, _block_rows(jb), _lane_group(g)] = _both_heads(r[:, :LANES]).astype(o_ref.dtype)
        tile = tiles.pop((sq, jb), None)
        if tile is None:
            tile = jnp.where(lane >= L_LANE, 1.0, 0.0)
        for half in range(2):
            tile = jnp.where(lane == 2 * g + half, m[_block_rows(half)], tile)
            tile = jnp.where(lane == L_LANE + 2 * g + half, r[_block_rows(half), LANES:], tile)
        if g == A_GROUPS - 1:
            st_ref[sq, _block_rows(jb), :] = tile
        else:
            tiles[(sq, jb)] = tile

    _pipeline(units, scores_of, probs_of, finish)


def _dilated_attn(q, k, v):
    n_seq, length, _ = q.shape
    blocks = min(ATTN_BLOCKS, length // BLOCK)
    seqs = ATTN_BLOCKS // blocks
    cur = pl.BlockSpec((seqs, blocks * BLOCK, A_WIDTH), lambda s, i: (s, i, 0))
    prev = pl.BlockSpec((seqs, BLOCK, A_WIDTH), lambda s, i: (s, jnp.maximum(i * blocks - 1, 0), 0))
    stat = pl.BlockSpec((seqs, blocks * BLOCK, LANES), lambda s, i: (s, i, 0))
    return pl.pallas_call(
        functools.partial(_dilated_attn_kernel, seqs=seqs, blocks=blocks),
        grid=(n_seq // seqs, length // (blocks * BLOCK)),
        in_specs=[pl.BlockSpec((2, 2 * BLOCK, 2 * BLOCK), lambda s, i: (0, 0, 0)),
                  cur, prev, cur, prev, cur],
        out_specs=[cur, stat],
        out_shape=[jax.ShapeDtypeStruct((n_seq, length, A_WIDTH), BF16),
                   jax.ShapeDtypeStruct((n_seq, length, LANES), F32)],
        compiler_params=pltpu.CompilerParams(
            dimension_semantics=("parallel", "arbitrary"), vmem_limit_bytes=VMEM_LIMIT),
        name="dilated_attn",
    )(_band_bias(A_MAX_DIST), q, k, k, v, v)


def _expand_heads(w):
    head_of_col = lax.broadcasted_iota(jnp.int32, (LANES, A_WIDTH), 1) // HEAD_DIM
    lane = lax.broadcasted_iota(jnp.int32, (LANES, A_WIDTH), 0)
    expand = (head_of_col == lane).astype(BF16)
    hi = w.astype(BF16)
    lo = (w - hi.astype(F32)).astype(BF16)
    return _dot(hi, expand) + _dot(lo, expand)


def _gated_conv(gb_ref, u_ref, u_tail_ref, cw_ref, part):
    u_cur = u_ref[0, part, :].astype(F32)
    if part.start == 0:
        u_prev = jnp.where(pl.program_id(1) > 0, u_tail_ref[0].astype(F32), 0.0)
    else:
        u_prev = u_ref[0, part.start - CONV_TAIL:part.start, :].astype(F32)
    head_rows = lax.broadcasted_iota(jnp.int32, (SUBLANES, CONV_CH), 0)
    acc = cw_ref[CONV_K - 1:CONV_K, :] * u_cur
    for shift in range(1, CONV_K):
        shifted = pltpu.roll(u_cur, shift, 0)
        from_prev = pltpu.roll(u_prev, shift, 0)[:SUBLANES]
        head = jnp.where(head_rows < shift, from_prev, shifted[:SUBLANES])
        shifted = jnp.concatenate([head, shifted[SUBLANES:]], axis=0)
        acc = acc + cw_ref[CONV_K - 1 - shift:CONV_K - shift, :] * shifted
    return gb_ref[0, part, :].astype(F32) * acc


def _mix_mlp_kernel(x_ref, o1_ref, o2_ref, o3_ref, s1_ref, s2_ref, s3_ref,
                    gb_ref, u_ref, u_tail_ref, yc_ref, cw_ref, gg_a_ref, gg_b_ref, wo_ref, g_mlp_ref, w1_ref, w2_ref,
                    g_next_ref, *refs, final):
    if final:
        out_ref, o_nat, s_nat, tmp_nat = refs
    else:
        out_ref, h_ref, o_nat, s_nat, tmp_nat = refs
    tm = x_ref.shape[1]
    quarter = tm // 4

    def to_token_order(dst, tmp, src4, src16):
        for b in range(4):
            dst[0][pl.ds(b, quarter, stride=4), :] = src4(b)
            for a in range(4):
                tmp[pl.ds(b * quarter + a, quarter // 4, stride=4), :] = src16(4 * a + b)
        for b in range(4):
            dst[1][pl.ds(b, quarter, stride=4), :] = tmp[b * quarter:(b + 1) * quarter, :]

    to_token_order((s_nat.at[0], s_nat.at[1]), tmp_nat, lambda r: s2_ref[0, r], lambda r: s3_ref[0, r])
    for g in range(A_GROUPS):
        to_token_order((o_nat.at[g], o_nat.at[A_GROUPS + g]), tmp_nat,
                       lambda r: o2_ref[0, r, :, _lane_group(g)].astype(F32),
                       lambda r: o3_ref[0, r, :, _lane_group(g)].astype(F32))

    parts = [slice(i * MLP_PART, (i + 1) * MLP_PART) for i in range(tm // MLP_PART)]

    def mixed_heads(part):
        outs = [o1_ref[0, part, :].astype(F32)] + [
            jnp.concatenate([o_nat[j * A_GROUPS + g, part, :] for g in range(A_GROUPS)], axis=1) for j in range(2)]
        tops = (s1_ref[0, part, :], s_nat[0, part, :], s_nat[1, part, :])
        sums = [pltpu.roll(t, LANES - L_LANE, 1) for t in tops]
        top = jnp.maximum(jnp.maximum(tops[0], tops[1]), tops[2])
        es = [jnp.exp2(m - top) for m in tops]
        inv = 1.0 / (es[0] * sums[0] + es[1] * sums[1] + es[2] * sums[2])
        is_head = lax.broadcasted_iota(jnp.int32, (MLP_PART, LANES), 1) < A_HEADS
        ya = jnp.zeros((MLP_PART, A_WIDTH), F32)
        for e, o in zip(es, outs):
            ya = ya + _expand_heads(jnp.where(is_head, e * inv, 0.0)) * o
        return (_rms_normalize(ya) * gg_a_ref[...]).astype(BF16)

    ya = [mixed_heads(part) for part in parts]
    yb = [(_rms_normalize(_gated_conv(gb_ref, u_ref, u_tail_ref, cw_ref, part)) * gg_b_ref[...]).astype(BF16)
          for part in parts]
    xs = [x_ref[0, part, :] + _dot(jnp.concatenate([a, b, yc_ref[0, part, :]], axis=1), wo_ref[...])
          for part, a, b in zip(parts, ya, yb)]
    hs = [(_rms_normalize(x) * g_mlp_ref[...]).astype(BF16) for x in xs]
    ffs = [jnp.zeros((MLP_PART, D_MODEL), F32) for _ in parts]
    for c in range(D_FF // FF_CHUNK):
        cols = slice(c * FF_CHUNK, (c + 1) * FF_CHUNK)
        for i, h in enumerate(hs):
            a = jnp.square(jnp.maximum(_dot(h, w1_ref[:, cols]), 0.0))
            ffs[i] = ffs[i] + _dot(a.astype(BF16), w2_ref[cols, :])
    for part, x, ff in zip(parts, xs, ffs):
        x = x + ff
        if final:
            out_ref[0, part, :] = _rms_normalize(x) * g_next_ref[...]
        else:
            out_ref[0, part, :] = x
            h_ref[0, part, :] = (_rms_normalize(x) * g_next_ref[...]).astype(BF16)


def _mix_mlp(x, outs, stats, gb, u, yc, conv_w, gg_a, gg_b, wo, g_mlp, w1, w2, g_next, final):
    batch, seq, _ = x.shape
    tm = MLP_TOKENS

    def rows(width):
        return pl.BlockSpec((1, tm, width), lambda b, i: (b, i, 0))

    def sub(dil, width):
        return pl.BlockSpec((1, dil, tm // dil, width), lambda b, i: (b, 0, i, 0))

    def const(shape):
        return pl.BlockSpec(shape, lambda b, i: (0, 0), pipeline_mode=pl.Buffered(1))

    tail = pl.BlockSpec((1, CONV_TAIL, CONV_CH), lambda b, i: (b, jnp.maximum(i * (tm // CONV_TAIL) - 1, 0), 0))
    return pl.pallas_call(
        functools.partial(_mix_mlp_kernel, final=final),
        grid=(batch, seq // tm),
        in_specs=[
            rows(D_MODEL), rows(A_WIDTH), sub(DILATIONS[1], A_WIDTH), sub(DILATIONS[2], A_WIDTH),
            rows(LANES), sub(DILATIONS[1], LANES), sub(DILATIONS[2], LANES),
            rows(CONV_CH), rows(CONV_CH), tail, rows(C_WIDTH),
            const((CONV_K, CONV_CH)), const((1, A_WIDTH)), const((1, CONV_CH)), const((MIX_WIDTH, D_MODEL)),
            const((1, D_MODEL)),
            const((D_MODEL, D_FF)), const((D_FF, D_MODEL)), const((1, D_MODEL)),
        ],
        out_specs=[rows(D_MODEL)] * (1 if final else 2),
        out_shape=[jax.ShapeDtypeStruct((batch, seq, D_MODEL), F32)] + (
            [] if final else [jax.ShapeDtypeStruct((batch, seq, D_MODEL), BF16)]),
        scratch_shapes=[pltpu.VMEM((2 * A_GROUPS, tm, LANES), F32), pltpu.VMEM((2, tm, LANES), F32),
                        pltpu.VMEM((tm, LANES), F32)],
        compiler_params=pltpu.CompilerParams(
            dimension_semantics=("parallel", "parallel"), vmem_limit_bytes=VMEM_LIMIT),
        name="mix_mlp",
    )(x, *outs, *stats, gb, u, u, yc, conv_w, gg_a, gg_b, wo, g_mlp, w1, w2, g_next)


def kernel(x, w_in, conv_w, sinks, g_mix, g_group, w_o, g_mlp, w_ff_in, w_ff_out, g_final):
    batch, seq, _ = x.shape
    depth = w_in.shape[0]
    qc_start = sum(IN_SPLITS[:6])
    c_start = A_WIDTH + CONV_CH
    q_scale = np.ones((IN_WIDTH,), np.float32)
    q_scale[0:A_WIDTH] = HEAD_DIM ** -0.5 * LOG2E
    q_scale[qc_start:qc_start + C_WIDTH] = HEAD_DIM ** -0.5 * LOG2E

    def reorder_c_heads(t, axis):
        heads = [lax.slice_in_dim(t, h * HEAD_DIM, (h + 1) * HEAD_DIM, axis=axis) for h in C_HEAD_ORDER]
        return jnp.concatenate(heads, axis=axis)

    def project_columns(w):
        w = w * q_scale
        w = jnp.concatenate([w[:, :qc_start], reorder_c_heads(w[:, qc_start:qc_start + C_WIDTH], 1),
                             w[:, qc_start + C_WIDTH:]], axis=1)
        groups = [lax.slice_in_dim(w, IN_START[name] + g * LANES, IN_START[name] + (g + 1) * LANES, axis=1)
                  for name, g in PROJ_GROUPS]
        return jnp.concatenate(groups, axis=1).astype(BF16)

    h = None
    for l in range(depth):
        final = l == depth - 1
        gg = g_group[l]
        (o1, st1, yc, q4, k4, v4, q16, k16, v16, gb, u) = _proj_attn(
            x if h is None else h, g_mix[l].reshape(1, D_MODEL), project_columns(w_in[l]),
            sinks[l].reshape(C_Q_HEADS), reorder_c_heads(gg[c_start:], 0).reshape(1, C_WIDTH),
            normalized=h is not None)

        outs, stats = [o1], [st1]
        for dil, (q, k, v) in zip(DILATIONS[1:], ((q4, k4, v4), (q16, k16, v16))):
            o, st = _dilated_attn(*(t.reshape(batch * dil, seq // dil, A_WIDTH) for t in (q, k, v)))
            outs.append(o.reshape(batch, dil, seq // dil, A_WIDTH))
            stats.append(st.reshape(batch, dil, seq // dil, LANES))

        w_o_l = jnp.concatenate([w_o[l][:c_start], reorder_c_heads(w_o[l][c_start:], 0)], axis=0).astype(BF16)
        g_next = g_final if final else g_mix[l + 1]
        res = _mix_mlp(x, outs, stats, gb, u, yc, conv_w[l], gg[:A_WIDTH].reshape(1, A_WIDTH),
                       gg[A_WIDTH:c_start].reshape(1, CONV_CH), w_o_l,
                       g_mlp[l].reshape(1, D_MODEL), w_ff_in[l].astype(BF16), w_ff_out[l].astype(BF16),
                       g_next.reshape(1, D_MODEL), final=final)
        x = res[0]
        h = None if final else res[1]
    return x
```

```python
import functools

import jax
import jax.numpy as jnp
import numpy as np
from jax import lax
from jax.experimental import pallas as pl
from jax.experimental.pallas import tpu as pltpu

D_MODEL = 1024
HEAD_DIM = 64
A_HEADS = 6
DILATIONS = (1, 4, 16)
A_MAX_DIST = 128
CONV_CH = 256
CONV_K = 3
C_Q_HEADS = 6
C_KV_HEADS = 2
C_GROUP = C_Q_HEADS // C_KV_HEADS
C_MAX_DIST = 127
BLOCK = 128
D_FF = 4 * D_MODEL
EPS = 1e-6

A_WIDTH = A_HEADS * HEAD_DIM
C_WIDTH = C_Q_HEADS * HEAD_DIM
KV_WIDTH = C_KV_HEADS * HEAD_DIM
MIX_WIDTH = A_WIDTH + CONV_CH + C_WIDTH
IN_NAMES = ("qa", "ka", "va", "gb", "gc", "xb", "qc", "kc", "vc")
IN_SPLITS = (A_WIDTH, A_WIDTH, A_WIDTH, CONV_CH, CONV_CH, CONV_CH, C_WIDTH, KV_WIDTH, KV_WIDTH)
IN_WIDTH = sum(IN_SPLITS)
IN_START = dict(zip(IN_NAMES, np.cumsum((0,) + IN_SPLITS[:-1]).tolist()))
LANES = 128
SUBLANES = 8
A_GROUPS = A_WIDTH // LANES
CONV_TAIL = 16
C_HEAD_ORDER = (0, 3, 1, 4, 2, 5)
ATTN_BLOCKS = 16
LOG2E = 1.4426950408889634
L_LANE = 64

PROJ_TOKENS = 1024
PROJ_PART = 256
PROJ_CHUNK = 512
PROJ_GROUPS = ([(name, g) for name in ("qa", "ka", "va") for g in range(3)]
               + [("gb", 0), ("gb", 1), ("qc", 0), ("gc", 0), ("xb", 0), ("gc", 1), ("xb", 1),
                  ("qc", 1), ("qc", 2), ("kc", 0), ("vc", 0)])
MLP_TOKENS = 512
MLP_PART = 256
FF_CHUNK = 1024
VMEM_LIMIT = 56 * 1024 * 1024

NEG = -1e30
BF16 = jnp.bfloat16
F32 = jnp.float32


def _rms_normalize(t):
    return t * lax.rsqrt(jnp.mean(t * t, axis=-1, keepdims=True) + EPS)


def _dot(a, b):
    return jnp.dot(a, b, preferred_element_type=F32)


def _dot_nt(a, b):
    return lax.dot_general(a, b, (((1,), (1,)), ((), ())), preferred_element_type=F32)


def _lane_group(g):
    return slice(g * LANES, (g + 1) * LANES)


LOCAL_NAMES = ("qa", "ka", "va", "qc", "kc", "vc")
LOCAL_WIDTHS = (A_WIDTH, A_WIDTH, A_WIDTH, C_WIDTH, KV_WIDTH, KV_WIDTH)
CARRY_NAMES = ("ka", "va", "kc", "vc")
CARRY_WIDTHS = (A_WIDTH, A_WIDTH, KV_WIDTH, KV_WIDTH)


def _proj_attn_kernel(sink_ref, x_ref, g_ref, w_ref, bias_a_ref, bias_c_ref, gg_c_ref, *refs, normalized):
    o1_ref, st1_ref, yc_ref = refs[0:3]
    strided = (refs[3:6], refs[6:9])
    gb_ref, u_ref = refs[9:11]
    h_ref = refs[11]
    local = dict(zip(LOCAL_NAMES, refs[12:18]))
    carry = dict(zip(CARRY_NAMES, refs[18:22]))
    stage = refs[22:22 + 3 * A_GROUPS]
    stage4 = refs[22 + 3 * A_GROUPS:]
    tm = x_ref.shape[1]
    blocks = tm // BLOCK

    @pl.when(pl.program_id(1) == 0)
    def _():
        for ref in carry.values():
            ref[...] = jnp.zeros(ref.shape, ref.dtype)

    parts = [slice(i * PROJ_PART, (i + 1) * PROJ_PART) for i in range(tm // PROJ_PART)]

    if not normalized:
        for part in parts:
            h_ref[part, :] = (_rms_normalize(x_ref[0, part, :]) * g_ref[...]).astype(BF16)

    def lhs(part):
        return x_ref[0, part, :] if normalized else h_ref[part, :]

    def write_subsequences(t):
        quarter = tm // 4
        tile = 2 * SUBLANES

        def residues(buf, first_row):
            halves = [[buf[pl.ds(first_row + 4 * SUBLANES * hf + r, SUBLANES, stride=4), :] for r in range(4)]
                      for hf in range(2)]
            return [jnp.concatenate([halves[0][r], halves[1][r]], axis=0) for r in range(4)]

        for g in range(A_GROUPS):
            buf, buf4 = stage[t * A_GROUPS + g], stage4[t * A_GROUPS + g]
            for j in range(quarter // tile):
                for b, rows in enumerate(residues(buf, 4 * tile * j)):
                    strided[0][t][0, b, tile * j:tile * (j + 1), _lane_group(g)] = rows.astype(BF16)
                    buf4[b * quarter + tile * j:b * quarter + tile * (j + 1), :] = rows
            for b in range(4):
                for j in range(quarter // 4 // tile):
                    for a, rows in enumerate(residues(buf4, b * quarter + 4 * tile * j)):
                        strided[1][t][0, 4 * a + b, tile * j:tile * (j + 1), _lane_group(g)] = rows.astype(BF16)

    staged = 0
    per_chunk = PROJ_CHUNK // LANES
    for c in range(IN_WIDTH // PROJ_CHUNK):
        for part in parts:
            z = _dot(lhs(part), w_ref[:, c * PROJ_CHUNK:(c + 1) * PROJ_CHUNK])
            gates = {}
            for k in range(per_chunk):
                j = c * per_chunk + k
                name, g = PROJ_GROUPS[j]
                zk = z[:, _lane_group(k)]
                if name == "gc":
                    gates[g] = zk
                elif name == "xb":
                    u_ref[0, part, _lane_group(g)] = (gates.pop(g) * zk).astype(BF16)
                elif name == "gb":
                    gb_ref[0, part, _lane_group(g)] = zk.astype(BF16)
                else:
                    local[name][0, part, _lane_group(g)] = zk.astype(BF16)
                    if j < 3 * A_GROUPS:
                        stage[j][part, :] = zk
        while staged < 3 and (staged + 1) * A_GROUPS <= (c + 1) * per_chunk:
            write_subsequences(staged)
            staged += 1

    lane = lax.broadcasted_iota(jnp.int32, (BLOCK, LANES), 1)
    first_bias = jnp.minimum(pl.program_id(1), 1)
    upper = lax.broadcasted_iota(jnp.int32, (2 * BLOCK, 1), 0) < BLOCK
    sinks = [jnp.where(upper, sink_ref[C_HEAD_ORDER[2 * g]], sink_ref[C_HEAD_ORDER[2 * g + 1]]) * LOG2E
             for g in range(C_WIDTH // LANES)]
    kv_lanes = slice(0, KV_WIDTH)
    units = [(kind, jb, g) for jb in range(blocks) for kind in ("a", "c") for g in range(A_GROUPS)]
    tiles, c_parts = {}, {}

    def scores_of(unit):
        kind, jb, g = unit
        bias_ref = bias_a_ref if kind == "a" else bias_c_ref
        bias = bias_ref[first_bias] if jb == 0 else bias_ref[1]
        if kind == "a":
            keys = _keys_of_block(carry["ka"], local["ka"], 0, jb, _lane_group(g))
            return _pair_scores(local["qa"][0, _block_rows(jb), _lane_group(g)], keys, bias)
        keys = _keys_of_block(carry["kc"], local["kc"], 0, jb, kv_lanes)
        return _pair_scores(local["qc"][0, _block_rows(jb), _lane_group(g)], keys, bias)

    def probs_of(unit, s):
        kind, _, g = unit
        return _pair_probs(s, None if kind == "a" else sinks[g])

    def finish(unit, probs):
        kind, jb, g = unit
        p, m = probs
        if kind == "a":
            r = _pair_values(p, _keys_of_block(carry["va"], local["va"], 0, jb, _lane_group(g)))
            o1_ref[0, _block_rows(jb), _lane_group(g)] = _both_heads(r[:, :LANES]).astype(o1_ref.dtype)
            tile = tiles.pop(jb, None)
            if tile is None:
                tile = jnp.where(lane >= L_LANE, 1.0, 0.0)
            for half in range(2):
                tile = jnp.where(lane == 2 * g + half, m[_block_rows(half)], tile)
                tile = jnp.where(lane == L_LANE + 2 * g + half, r[_block_rows(half), LANES:], tile)
            if g == A_GROUPS - 1:
                st1_ref[0, _block_rows(jb), :] = tile
            else:
                tiles[jb] = tile
        else:
            r = _pair_values(p, _keys_of_block(carry["vc"], local["vc"], 0, jb, kv_lanes))
            denom = r[:, LANES:] + jnp.exp2(sinks[g] - m)
            c_parts.setdefault(jb, []).append(_both_heads(r[:, :LANES]) / _both_heads(denom))
            if g == A_GROUPS - 1:
                oc = jnp.concatenate(c_parts.pop(jb), axis=1)
                yc_ref[0, _block_rows(jb), :] = (_rms_normalize(oc) * gg_c_ref[...]).astype(yc_ref.dtype)

    _pipeline(units, scores_of, probs_of, finish)

    for name in CARRY_NAMES:
        carry[name][0] = local[name][0, _block_rows(blocks - 1), :]


def _proj_attn(x, g, w, sinks, gg_c, normalized):
    batch, seq, _ = x.shape
    tm = PROJ_TOKENS

    def rows(width):
        return pl.BlockSpec((1, tm, width), lambda b, i: (b, i, 0))

    def sub(dil):
        return pl.BlockSpec((1, dil, tm // dil, A_WIDTH), lambda b, i: (b, 0, i, 0))

    def const(shape):
        return pl.BlockSpec(shape, lambda b, i: (0,) * len(shape), pipeline_mode=pl.Buffered(1))

    out_specs = [rows(A_WIDTH), rows(LANES), rows(C_WIDTH)]
    out_shape = [jax.ShapeDtypeStruct((batch, seq, A_WIDTH), BF16), jax.ShapeDtypeStruct((batch, seq, LANES), F32),
                 jax.ShapeDtypeStruct((batch, seq, C_WIDTH), BF16)]
    for dil in DILATIONS[1:]:
        out_specs += [sub(dil)] * 3
        out_shape += [jax.ShapeDtypeStruct((batch, dil, seq // dil, A_WIDTH), BF16)] * 3
    out_specs += [rows(CONV_CH)] * 2
    out_shape += [jax.ShapeDtypeStruct((batch, seq, CONV_CH), BF16)] * 2
    scratch = ([pltpu.VMEM((tm, D_MODEL), BF16)]
               + [pltpu.VMEM((1, tm, wd), BF16) for wd in LOCAL_WIDTHS]
               + [pltpu.VMEM((1, BLOCK, wd), BF16) for wd in CARRY_WIDTHS]
               + [pltpu.VMEM((tm, LANES), F32)] * (6 * A_GROUPS))
    bias_spec = const((2, 2 * BLOCK, 2 * BLOCK))
    return pl.pallas_call(
        functools.partial(_proj_attn_kernel, normalized=normalized),
        grid=(batch, seq // tm),
        in_specs=[pl.BlockSpec(memory_space=pltpu.SMEM), rows(D_MODEL), const((1, D_MODEL)),
                  const((D_MODEL, IN_WIDTH)), bias_spec, bias_spec, const((1, C_WIDTH))],
        out_specs=out_specs,
        out_shape=out_shape,
        scratch_shapes=scratch,
        compiler_params=pltpu.CompilerParams(
            dimension_semantics=("parallel", "arbitrary"), vmem_limit_bytes=VMEM_LIMIT),
        name="proj_attn",
    )(sinks, x, g, w, _band_bias(A_MAX_DIST), _band_bias(C_MAX_DIST), gg_c)


def _band_bias(max_dist):
    qi = np.arange(2 * BLOCK)[:, None] % BLOCK
    kj = np.arange(2 * BLOCK)[None, :]
    dist = BLOCK + qi - kj
    band = (dist >= 0) & (dist <= max_dist)
    first = band & (kj >= BLOCK)
    return jnp.asarray(np.where(np.stack([first, band]), 0.0, NEG), F32)


def _low_half():
    return lax.broadcasted_iota(jnp.int32, (BLOCK, LANES), 1) < HEAD_DIM


def _pair_scores(q2, k2, bias):
    low = _low_half()
    zero = jnp.zeros_like(q2)
    stacked = jnp.concatenate([jnp.where(low, q2, zero), jnp.where(low, zero, q2)], axis=0)
    return _dot_nt(stacked, k2) + bias


def _pair_probs(s, sink=None):
    m = jnp.max(s, axis=-1, keepdims=True)
    if sink is not None:
        m = jnp.maximum(m, sink)
    return jnp.exp2(s - m).astype(BF16), m


def _pair_values(p, v2):
    ones = jnp.ones((2 * BLOCK, LANES), BF16)
    return _dot(p, jnp.concatenate([v2, ones], axis=1))


def _both_heads(t):
    return jnp.where(_low_half(), t[:BLOCK], t[BLOCK:])


def _pipeline(units, scores_of, probs_of, finish):
    n = len(units)
    scores, probs = {}, {}
    for t in range(n + 2):
        if t >= 2:
            finish(units[t - 2], probs.pop(t - 2))
        if 1 <= t <= n:
            probs[t - 1] = probs_of(units[t - 1], scores.pop(t - 1))
        if t < n:
            scores[t] = scores_of(units[t])


def _block_rows(jb):
    return slice(jb * BLOCK, (jb + 1) * BLOCK)


def _keys_of_block(prev_ref, cur_ref, sq, jb, cols):
    before = prev_ref[sq, :, cols] if jb == 0 else cur_ref[sq, _block_rows(jb - 1), cols]
    return jnp.concatenate([before, cur_ref[sq, _block_rows(jb), cols]], axis=0)


def _dilated_attn_kernel(bias_ref, q_ref, kp_ref, kc_ref, vp_ref, vc_ref, o_ref, st_ref, *, seqs, blocks):
    lane = lax.broadcasted_iota(jnp.int32, (BLOCK, LANES), 1)
    first_bias = jnp.minimum(pl.program_id(1), 1)
    units = [(sq, jb, g) for sq in range(seqs) for jb in range(blocks) for g in range(A_GROUPS)]
    tiles = {}

    def scores_of(unit):
        sq, jb, g = unit
        bias = bias_ref[first_bias] if jb == 0 else bias_ref[1]
        return _pair_scores(q_ref[sq, _block_rows(jb), _lane_group(g)],
                            _keys_of_block(kp_ref, kc_ref, sq, jb, _lane_group(g)), bias)

    def probs_of(unit, s):
        return _pair_probs(s)

    def finish(unit, probs):
        sq, jb, g = unit
        p, m = probs
        r = _pair_values(p, _keys_of_block(vp_ref, vc_ref, sq, jb, _lane_group(g)))
        o_ref[sq, _block_rows(jb), _lane_group(g)] = _both_heads(r[:, :LANES]).astype(o_ref.dtype)
        tile = tiles.pop((sq, jb), None)
        if tile is None:
            tile = jnp.where(lane >= L_LANE, 1.0, 0.0)
        for half in range(2):
            tile = jnp.where(lane == 2 * g + half, m[_block_rows(half)], tile)
            tile = jnp.where(lane == L_LANE + 2 * g + half, r[_block_rows(half), LANES:], tile)
        if g == A_GROUPS - 1:
            st_ref[sq, _block_rows(jb), :] = tile
        else:
            tiles[(sq, jb)] = tile

    _pipeline(units, scores_of, probs_of, finish)


def _dilated_attn(q, k, v):
    n_seq, length, _ = q.shape
    blocks = min(ATTN_BLOCKS, length // BLOCK)
    seqs = ATTN_BLOCKS // blocks
    cur = pl.BlockSpec((seqs, blocks * BLOCK, A_WIDTH), lambda s, i: (s, i, 0))
    prev = pl.BlockSpec((seqs, BLOCK, A_WIDTH), lambda s, i: (s, jnp.maximum(i * blocks - 1, 0), 0))
    stat = pl.BlockSpec((seqs, blocks * BLOCK, LANES), lambda s, i: (s, i, 0))
    return pl.pallas_call(
        functools.partial(_dilated_attn_kernel, seqs=seqs, blocks=blocks),
        grid=(n_seq // seqs, length // (blocks * BLOCK)),
        in_specs=[pl.BlockSpec((2, 2 * BLOCK, 2 * BLOCK), lambda s, i: (0, 0, 0)),
                  cur, prev, cur, prev, cur],
        out_specs=[cur, stat],
        out_shape=[jax.ShapeDtypeStruct((n_seq, length, A_WIDTH), BF16),
                   jax.ShapeDtypeStruct((n_seq, length, LANES), F32)],
        compiler_params=pltpu.CompilerParams(
            dimension_semantics=("parallel", "arbitrary"), vmem_limit_bytes=VMEM_LIMIT),
        name="dilated_attn",
    )(_band_bias(A_MAX_DIST), q, k, k, v, v)


def _expand_heads(w):
    head_of_col = lax.broadcasted_iota(jnp.int32, (LANES, A_WIDTH), 1) // HEAD_DIM
    lane = lax.broadcasted_iota(jnp.int32, (LANES, A_WIDTH), 0)
    expand = (head_of_col == lane).astype(BF16)
    hi = w.astype(BF16)
    lo = (w - hi.astype(F32)).astype(BF16)
    return _dot(hi, expand) + _dot(lo, expand)


def _gated_conv(gb_ref, u_ref, u_tail_ref, cw_ref, part):
    u_cur = u_ref[0, part, :].astype(F32)
    if part.start == 0:
        u_prev = jnp.where(pl.program_id(1) > 0, u_tail_ref[0].astype(F32), 0.0)
    else:
        u_prev = u_ref[0, part.start - CONV_TAIL:part.start, :].astype(F32)
    head_rows = lax.broadcasted_iota(jnp.int32, (SUBLANES, CONV_CH), 0)
    acc = cw_ref[CONV_K - 1:CONV_K, :] * u_cur
    for shift in range(1, CONV_K):
        shifted = pltpu.roll(u_cur, shift, 0)
        from_prev = pltpu.roll(u_prev, shift, 0)[:SUBLANES]
        head = jnp.where(head_rows < shift, from_prev, shifted[:SUBLANES])
        shifted = jnp.concatenate([head, shifted[SUBLANES:]], axis=0)
        acc = acc + cw_ref[CONV_K - 1 - shift:CONV_K - shift, :] * shifted
    return gb_ref[0, part, :].astype(F32) * acc


def _mix_mlp_kernel(x_ref, o1_ref, o2_ref, o3_ref, s1_ref, s2_ref, s3_ref,
                    gb_ref, u_ref, u_tail_ref, yc_ref, cw_ref, gg_a_ref, gg_b_ref, wo_ref, g_mlp_ref, w1_ref, w2_ref,
                    g_next_ref, *refs, final):
    if final:
        out_ref, o_nat, s_nat, tmp_nat = refs
    else:
        out_ref, h_ref, o_nat, s_nat, tmp_nat = refs
    tm = x_ref.shape[1]
    quarter = tm // 4

    def to_token_order(dst, tmp, src4, src16):
        for b in range(4):
            dst[0][pl.ds(b, quarter, stride=4), :] = src4(b)
            for a in range(4):
                tmp[pl.ds(b * quarter + a, quarter // 4, stride=4), :] = src16(4 * a + b)
        for b in range(4):
            dst[1][pl.ds(b, quarter, stride=4), :] = tmp[b * quarter:(b + 1) * quarter, :]

    to_token_order((s_nat.at[0], s_nat.at[1]), tmp_nat, lambda r: s2_ref[0, r], lambda r: s3_ref[0, r])
    for g in range(A_GROUPS):
        to_token_order((o_nat.at[g], o_nat.at[A_GROUPS + g]), tmp_nat,
                       lambda r: o2_ref[0, r, :, _lane_group(g)].astype(F32),
                       lambda r: o3_ref[0, r, :, _lane_group(g)].astype(F32))

    parts = [slice(i * MLP_PART, (i + 1) * MLP_PART) for i in range(tm // MLP_PART)]

    def mixed_heads(part):
        outs = [o1_ref[0, part, :].astype(F32)] + [
            jnp.concatenate([o_nat[j * A_GROUPS + g, part, :] for g in range(A_GROUPS)], axis=1) for j in range(2)]
        tops = (s1_ref[0, part, :], s_nat[0, part, :], s_nat[1, part, :])
        sums = [pltpu.roll(t, LANES - L_LANE, 1) for t in tops]
        top = jnp.maximum(jnp.maximum(tops[0], tops[1]), tops[2])
        es = [jnp.exp2(m - top) for m in tops]
        inv = 1.0 / (es[0] * sums[0] + es[1] * sums[1] + es[2] * sums[2])
        is_head = lax.broadcasted_iota(jnp.int32, (MLP_PART, LANES), 1) < A_HEADS
        ya = jnp.zeros((MLP_PART, A_WIDTH), F32)
        for e, o in zip(es, outs):
            ya = ya + _expand_heads(jnp.where(is_head, e * inv, 0.0)) * o
        return (_rms_normalize(ya) * gg_a_ref[...]).astype(BF16)

    ya = [mixed_heads(part) for part in parts]
    yb = [(_rms_normalize(_gated_conv(gb_ref, u_ref, u_tail_ref, cw_ref, part)) * gg_b_ref[...]).astype(BF16)
          for part in parts]
    xs = [x_ref[0, part, :] + _dot(jnp.concatenate([a, b, yc_ref[0, part, :]], axis=1), wo_ref[...])
          for part, a, b in zip(parts, ya, yb)]
    hs = [(_rms_normalize(x) * g_mlp_ref[...]).astype(BF16) for x in xs]
    ffs = [jnp.zeros((MLP_PART, D_MODEL), F32) for _ in parts]
    for c in range(D_FF // FF_CHUNK):
        cols = slice(c * FF_CHUNK, (c + 1) * FF_CHUNK)
        for i, h in enumerate(hs):
            a = jnp.square(jnp.maximum(_dot(h, w1_ref[:, cols]), 0.0))
            ffs[i] = ffs[i] + _dot(a.astype(BF16), w2_ref[cols, :])
    for part, x, ff in zip(parts, xs, ffs):
        x = x + ff
        if final:
            out_ref[0, part, :] = _rms_normalize(x) * g_next_ref[...]
        else:
            out_ref[0, part, :] = x
            h_ref[0, part, :] = (_rms_normalize(x) * g_next_ref[...]).astype(BF16)


def _mix_mlp(x, outs, stats, gb, u, yc, conv_w, gg_a, gg_b, wo, g_mlp, w1, w2, g_next, final):
    batch, seq, _ = x.shape
    tm = MLP_TOKENS

    def rows(width):
        return pl.BlockSpec((1, tm, width), lambda b, i: (b, i, 0))

    def sub(dil, width):
        return pl.BlockSpec((1, dil, tm // dil, width), lambda b, i: (b, 0, i, 0))

    def const(shape):
        return pl.BlockSpec(shape, lambda b, i: (0, 0), pipeline_mode=pl.Buffered(1))

    tail = pl.BlockSpec((1, CONV_TAIL, CONV_CH), lambda b, i: (b, jnp.maximum(i * (tm // CONV_TAIL) - 1, 0), 0))
    return pl.pallas_call(
        functools.partial(_mix_mlp_kernel, final=final),
        grid=(batch, seq // tm),
        in_specs=[
            rows(D_MODEL), rows(A_WIDTH), sub(DILATIONS[1], A_WIDTH), sub(DILATIONS[2], A_WIDTH),
            rows(LANES), sub(DILATIONS[1], LANES), sub(DILATIONS[2], LANES),
            rows(CONV_CH), rows(CONV_CH), tail, rows(C_WIDTH),
            const((CONV_K, CONV_CH)), const((1, A_WIDTH)), const((1, CONV_CH)), const((MIX_WIDTH, D_MODEL)),
            const((1, D_MODEL)),
            const((D_MODEL, D_FF)), const((D_FF, D_MODEL)), const((1, D_MODEL)),
        ],
        out_specs=[rows(D_MODEL)] * (1 if final else 2),
        out_shape=[jax.ShapeDtypeStruct((batch, seq, D_MODEL), F32)] + (
            [] if final else [jax.ShapeDtypeStruct((batch, seq, D_MODEL), BF16)]),
        scratch_shapes=[pltpu.VMEM((2 * A_GROUPS, tm, LANES), F32), pltpu.VMEM((2, tm, LANES), F32),
                        pltpu.VMEM((tm, LANES), F32)],
        compiler_params=pltpu.CompilerParams(
            dimension_semantics=("parallel", "parallel"), vmem_limit_bytes=VMEM_LIMIT),
        name="mix_mlp",
    )(x, *outs, *stats, gb, u, u, yc, conv_w, gg_a, gg_b, wo, g_mlp, w1, w2, g_next)


def kernel(x, w_in, conv_w, sinks, g_mix, g_group, w_o, g_mlp, w_ff_in, w_ff_out, g_final):
    batch, seq, _ = x.shape
    depth = w_in.shape[0]
    qc_start = sum(IN_SPLITS[:6])
    c_start = A_WIDTH + CONV_CH
    q_scale = np.ones((IN_WIDTH,), np.float32)
    q_scale[0:A_WIDTH] = HEAD_DIM ** -0.5 * LOG2E
    q_scale[qc_start:qc_start + C_WIDTH] = HEAD_DIM ** -0.5 * LOG2E

    def reorder_c_heads(t, axis):
        heads = [lax.slice_in_dim(t, h * HEAD_DIM, (h + 1) * HEAD_DIM, axis=axis) for h in C_HEAD_ORDER]
        return jnp.concatenate(heads, axis=axis)

    def project_columns(w):
        w = w * q_scale
        w = jnp.concatenate([w[:, :qc_start], reorder_c_heads(w[:, qc_start:qc_start + C_WIDTH], 1),
                             w[:, qc_start + C_WIDTH:]], axis=1)
        groups = [lax.slice_in_dim(w, IN_START[name] + g * LANES, IN_START[name] + (g + 1) * LANES, axis=1)
                  for name, g in PROJ_GROUPS]
        return jnp.concatenate(groups, axis=1).astype(BF16)

    h = None
    for l in range(depth):
        final = l == depth - 1
        gg = g_group[l]
        (o1, st1, yc, q4, k4, v4, q16, k16, v16, gb, u) = _proj_attn(
            x if h is None else h, g_mix[l].reshape(1, D_MODEL), project_columns(w_in[l]),
            sinks[l].reshape(C_Q_HEADS), reorder_c_heads(gg[c_start:], 0).reshape(1, C_WIDTH),
            normalized=h is not None)

        outs, stats = [o1], [st1]
        for dil, (q, k, v) in zip(DILATIONS[1:], ((q4, k4, v4), (q16, k16, v16))):
            o, st = _dilated_attn(*(t.reshape(batch * dil, seq // dil, A_WIDTH) for t in (q, k, v)))
            outs.append(o.reshape(batch, dil, seq // dil, A_WIDTH))
            stats.append(st.reshape(batch, dil, seq // dil, LANES))

        w_o_l = jnp.concatenate([w_o[l][:c_start], reorder_c_heads(w_o[l][c_start:], 0)], axis=0).astype(BF16)
        g_next = g_final if final else g_mix[l + 1]
        res = _mix_mlp(x, outs, stats, gb, u, yc, conv_w[l], gg[:A_WIDTH].reshape(1, A_WIDTH),
                       gg[A_WIDTH:c_start].reshape(1, CONV_CH), w_o_l,
                       g_mlp[l].reshape(1, D_MODEL), w_ff_in[l].astype(BF16), w_ff_out[l].astype(BF16),
                       g_next.reshape(1, D_MODEL), final=final)
        x = res[0]
        h = None if final else res[1]
    return x
```

```python
import functools

import jax
import jax.numpy as jnp
import numpy as np
from jax import lax
from jax.experimental import pallas as pl
from jax.experimental.pallas import tpu as pltpu

D_MODEL = 1024
HEAD_DIM = 64
A_HEADS = 6
DILATIONS = (1, 4, 16)
A_MAX_DIST = 128
CONV_CH = 256
CONV_K = 3
C_Q_HEADS = 6
C_KV_HEADS = 2
C_GROUP = C_Q_HEADS // C_KV_HEADS
C_MAX_DIST = 127
BLOCK = 128
D_FF = 4 * D_MODEL
EPS = 1e-6

A_WIDTH = A_HEADS * HEAD_DIM
C_WIDTH = C_Q_HEADS * HEAD_DIM
KV_WIDTH = C_KV_HEADS * HEAD_DIM
MIX_WIDTH = A_WIDTH + CONV_CH + C_WIDTH
IN_NAMES = ("qa", "ka", "va", "gb", "gc", "xb", "qc", "kc", "vc")
IN_SPLITS = (A_WIDTH, A_WIDTH, A_WIDTH, CONV_CH, CONV_CH, CONV_CH, C_WIDTH, KV_WIDTH, KV_WIDTH)
IN_WIDTH = sum(IN_SPLITS)
IN_START = dict(zip(IN_NAMES, np.cumsum((0,) + IN_SPLITS[:-1]).tolist()))
LANES = 128
SUBLANES = 8
A_GROUPS = A_WIDTH // LANES
CONV_TAIL = 16
C_HEAD_ORDER = (0, 3, 1, 4, 2, 5)
ATTN_BLOCKS = 16
LOG2E = 1.4426950408889634
L_LANE = 64

PROJ_TOKENS = 1024
PROJ_PART = 256
PROJ_CHUNK = 512
PROJ_GROUPS = ([(name, g) for name in ("qa", "ka", "va") for g in range(3)]
               + [("gb", 0), ("gb", 1), ("qc", 0), ("gc", 0), ("xb", 0), ("gc", 1), ("xb", 1),
                  ("qc", 1), ("qc", 2), ("kc", 0), ("vc", 0)])
MLP_TOKENS = 512
MLP_PART = 256
FF_CHUNK = 1024
VMEM_LIMIT = 56 * 1024 * 1024

NEG = -1e30
BF16 = jnp.bfloat16
F32 = jnp.float32


def _rms_normalize(t):
    return t * lax.rsqrt(jnp.mean(t * t, axis=-1, keepdims=True) + EPS)


def _dot(a, b):
    return jnp.dot(a, b, preferred_element_type=F32)


def _dot_nt(a, b):
    return lax.dot_general(a, b, (((1,), (1,)), ((), ())), preferred_element_type=F32)


def _lane_group(g):
    return slice(g * LANES, (g + 1) * LANES)


LOCAL_NAMES = ("qa", "ka", "va", "qc", "kc", "vc")
LOCAL_WIDTHS = (A_WIDTH, A_WIDTH, A_WIDTH, C_WIDTH, KV_WIDTH, KV_WIDTH)
CARRY_NAMES = ("ka", "va", "kc", "vc")
CARRY_WIDTHS = (A_WIDTH, A_WIDTH, KV_WIDTH, KV_WIDTH)


def _proj_attn_kernel(sink_ref, x_ref, g_ref, w_ref, bias_a_ref, bias_c_ref, perm_ref, gg_c_ref, *refs, normalized):
    o1_ref, st1_ref, yc_ref = refs[0:3]
    strided = (refs[3:6], refs[6:9])
    gb_ref, u_ref = refs[9:11]
    h_ref = refs[11]
    local = dict(zip(LOCAL_NAMES, refs[12:18]))
    carry = dict(zip(CARRY_NAMES, refs[18:22]))
    tm = x_ref.shape[1]
    blocks = tm // BLOCK

    @pl.when(pl.program_id(1) == 0)
    def _():
        for ref in carry.values():
            ref[...] = jnp.zeros(ref.shape, ref.dtype)

    parts = [slice(i * PROJ_PART, (i + 1) * PROJ_PART) for i in range(tm // PROJ_PART)]

    if not normalized:
        for part in parts:
            h_ref[part, :] = (_rms_normalize(x_ref[0, part, :]) * g_ref[...]).astype(BF16)

    def lhs(part):
        return x_ref[0, part, :] if normalized else h_ref[part, :]

    per_chunk = PROJ_CHUNK // LANES
    for c in range(IN_WIDTH // PROJ_CHUNK):
        for part in parts:
            z = _dot(lhs(part), w_ref[:, c * PROJ_CHUNK:(c + 1) * PROJ_CHUNK])
            gates = {}
            for k in range(per_chunk):
                name, g = PROJ_GROUPS[c * per_chunk + k]
                zk = z[:, _lane_group(k)]
                if name == "gc":
                    gates[g] = zk
                elif name == "xb":
                    u_ref[0, part, _lane_group(g)] = (gates.pop(g) * zk).astype(BF16)
                elif name == "gb":
                    gb_ref[0, part, _lane_group(g)] = zk.astype(BF16)
                else:
                    local[name][0, part, _lane_group(g)] = zk.astype(BF16)

    for i, part in enumerate(parts):
        qkv = jnp.concatenate([local[name][0, part, :] for name in ("qa", "ka", "va")], axis=1)
        for d, dil in enumerate(DILATIONS[1:]):
            n = PROJ_PART // dil
            moved = _dot(perm_ref[d], qkv).astype(BF16)
            for r in range(dil):
                for t in range(3):
                    strided[d][t][0, r, i * n:(i + 1) * n, :] = moved[r * n:(r + 1) * n, t * A_WIDTH:(t + 1) * A_WIDTH]

    lane = lax.broadcasted_iota(jnp.int32, (BLOCK, LANES), 1)
    first_bias = jnp.minimum(pl.program_id(1), 1)
    upper = lax.broadcasted_iota(jnp.int32, (2 * BLOCK, 1), 0) < BLOCK
    sinks = [jnp.where(upper, sink_ref[C_HEAD_ORDER[2 * g]], sink_ref[C_HEAD_ORDER[2 * g + 1]]) * LOG2E
             for g in range(C_WIDTH // LANES)]
    kv_lanes = slice(0, KV_WIDTH)
    units = [(kind, jb, g) for jb in range(blocks) for kind in ("a", "c") for g in range(A_GROUPS)]
    tiles, c_parts = {}, {}

    def scores_of(unit):
        kind, jb, g = unit
        bias_ref = bias_a_ref if kind == "a" else bias_c_ref
        bias = bias_ref[first_bias] if jb == 0 else bias_ref[1]
        if kind == "a":
            keys = _keys_of_block(carry["ka"], local["ka"], 0, jb, _lane_group(g))
            return _pair_scores(local["qa"][0, _block_rows(jb), _lane_group(g)], keys, bias)
        keys = _keys_of_block(carry["kc"], local["kc"], 0, jb, kv_lanes)
        return _pair_scores(local["qc"][0, _block_rows(jb), _lane_group(g)], keys, bias)

    def probs_of(unit, s):
        kind, _, g = unit
        return _pair_probs(s, None if kind == "a" else sinks[g])

    def finish(unit, probs):
        kind, jb, g = unit
        p, m = probs
        if kind == "a":
            r = _pair_values(p, _keys_of_block(carry["va"], local["va"], 0, jb, _lane_group(g)))
            o1_ref[0, _block_rows(jb), _lane_group(g)] = _both_heads(r[:, :LANES]).astype(o1_ref.dtype)
            tile = tiles.pop(jb, None)
            if tile is None:
                tile = jnp.where(lane >= L_LANE, 1.0, 0.0)
            for half in range(2):
                tile = jnp.where(lane == 2 * g + half, m[_block_rows(half)], tile)
                tile = jnp.where(lane == L_LANE + 2 * g + half, r[_block_rows(half), LANES:], tile)
            if g == A_GROUPS - 1:
                st1_ref[0, _block_rows(jb), :] = tile
            else:
                tiles[jb] = tile
        else:
            r = _pair_values(p, _keys_of_block(carry["vc"], local["vc"], 0, jb, kv_lanes))
            denom = r[:, LANES:] + jnp.exp2(sinks[g] - m)
            c_parts.setdefault(jb, []).append(_both_heads(r[:, :LANES]) / _both_heads(denom))
            if g == A_GROUPS - 1:
                oc = jnp.concatenate(c_parts.pop(jb), axis=1)
                yc_ref[0, _block_rows(jb), :] = (_rms_normalize(oc) * gg_c_ref[...]).astype(yc_ref.dtype)

    _pipeline(units, scores_of, probs_of, finish)

    for name in CARRY_NAMES:
        carry[name][0] = local[name][0, _block_rows(blocks - 1), :]


def _residue_permutations():
    mats = np.zeros((len(DILATIONS) - 1, PROJ_PART, PROJ_PART), np.float32)
    for d, dil in enumerate(DILATIONS[1:]):
        n = PROJ_PART // dil
        for r in range(dil):
            mats[d, r * n + np.arange(n), dil * np.arange(n) + r] = 1.0
    return jnp.asarray(mats, BF16)


def _proj_attn(x, g, w, sinks, gg_c, normalized):
    batch, seq, _ = x.shape
    tm = PROJ_TOKENS

    def rows(width):
        return pl.BlockSpec((1, tm, width), lambda b, i: (b, i, 0))

    def sub(dil):
        return pl.BlockSpec((1, dil, tm // dil, A_WIDTH), lambda b, i: (b, 0, i, 0))

    def const(shape):
        return pl.BlockSpec(shape, lambda b, i: (0,) * len(shape), pipeline_mode=pl.Buffered(1))

    out_specs = [rows(A_WIDTH), rows(LANES), rows(C_WIDTH)]
    out_shape = [jax.ShapeDtypeStruct((batch, seq, A_WIDTH), BF16), jax.ShapeDtypeStruct((batch, seq, LANES), F32),
                 jax.ShapeDtypeStruct((batch, seq, C_WIDTH), BF16)]
    for dil in DILATIONS[1:]:
        out_specs += [sub(dil)] * 3
        out_shape += [jax.ShapeDtypeStruct((batch, dil, seq // dil, A_WIDTH), BF16)] * 3
    out_specs += [rows(CONV_CH)] * 2
    out_shape += [jax.ShapeDtypeStruct((batch, seq, CONV_CH), BF16)] * 2
    scratch = ([pltpu.VMEM((tm, D_MODEL), BF16)]
               + [pltpu.VMEM((1, tm, wd), BF16) for wd in LOCAL_WIDTHS]
               + [pltpu.VMEM((1, BLOCK, wd), BF16) for wd in CARRY_WIDTHS])
    bias_spec = const((2, 2 * BLOCK, 2 * BLOCK))
    return pl.pallas_call(
        functools.partial(_proj_attn_kernel, normalized=normalized),
        grid=(batch, seq // tm),
        in_specs=[pl.BlockSpec(memory_space=pltpu.SMEM), rows(D_MODEL), const((1, D_MODEL)),
                  const((D_MODEL, IN_WIDTH)), bias_spec, bias_spec,
                  const((len(DILATIONS) - 1, PROJ_PART, PROJ_PART)), const((1, C_WIDTH))],
        out_specs=out_specs,
        out_shape=out_shape,
        scratch_shapes=scratch,
        compiler_params=pltpu.CompilerParams(
            dimension_semantics=("parallel", "arbitrary"), vmem_limit_bytes=VMEM_LIMIT),
        name="proj_attn",
    )(sinks, x, g, w, _band_bias(A_MAX_DIST), _band_bias(C_MAX_DIST), _residue_permutations(), gg_c)


def _band_bias(max_dist):
    qi = np.arange(2 * BLOCK)[:, None] % BLOCK
    kj = np.arange(2 * BLOCK)[None, :]
    dist = BLOCK + qi - kj
    band = (dist >= 0) & (dist <= max_dist)
    first = band & (kj >= BLOCK)
    return jnp.asarray(np.where(np.stack([first, band]), 0.0, NEG), F32)


def _low_half():
    return lax.broadcasted_iota(jnp.int32, (BLOCK, LANES), 1) < HEAD_DIM


def _pair_scores(q2, k2, bias):
    low = _low_half()
    zero = jnp.zeros_like(q2)
    stacked = jnp.concatenate([jnp.where(low, q2, zero), jnp.where(low, zero, q2)], axis=0)
    return _dot_nt(stacked, k2) + bias


def _pair_probs(s, sink=None):
    m = jnp.max(s, axis=-1, keepdims=True)
    if sink is not None:
        m = jnp.maximum(m, sink)
    return jnp.exp2(s - m).astype(BF16), m


def _pair_values(p, v2):
    ones = jnp.ones((2 * BLOCK, LANES), BF16)
    return _dot(p, jnp.concatenate([v2, ones], axis=1))


def _both_heads(t):
    return jnp.where(_low_half(), t[:BLOCK], t[BLOCK:])


def _pipeline(units, scores_of, probs_of, finish):
    n = len(units)
    scores, probs = {}, {}
    for t in range(n + 2):
        if t >= 2:
            finish(units[t - 2], probs.pop(t - 2))
        if 1 <= t <= n:
            probs[t - 1] = probs_of(units[t - 1], scores.pop(t - 1))
        if t < n:
            scores[t] = scores_of(units[t])


def _block_rows(jb):
    return slice(jb * BLOCK, (jb + 1) * BLOCK)


def _keys_of_block(prev_ref, cur_ref, sq, jb, cols):
    before = prev_ref[sq, :, cols] if jb == 0 else cur_ref[sq, _block_rows(jb - 1), cols]
    return jnp.concatenate([before, cur_ref[sq, _block_rows(jb), cols]], axis=0)


def _dilated_attn_kernel(bias_ref, q_ref, kp_ref, kc_ref, vp_ref, vc_ref, o_ref, st_ref, *, seqs, blocks):
    lane = lax.broadcasted_iota(jnp.int32, (BLOCK, LANES), 1)
    first_bias = jnp.minimum(pl.program_id(1), 1)
    units = [(sq, jb, g) for sq in range(seqs) for jb in range(blocks) for g in range(A_GROUPS)]
    tiles = {}

    def scores_of(unit):
        sq, jb, g = unit
        bias = bias_ref[first_bias] if jb == 0 else bias_ref[1]
        return _pair_scores(q_ref[sq, _block_rows(jb), _lane_group(g)],
                            _keys_of_block(kp_ref, kc_ref, sq, jb, _lane_group(g)), bias)

    def probs_of(unit, s):
        return _pair_probs(s)

    def finish(unit, probs):
        sq, jb, g = unit
        p, m = probs
        r = _pair_values(p, _keys_of_block(vp_ref, vc_ref, sq, jb, _lane_group(g)))
        o_ref[sq, _block_rows(jb), _lane_group(g)] = _both_heads(r[:, :LANES]).astype(o_ref.dtype)
        tile = tiles.pop((sq, jb), None)
        if tile is None:
            tile = jnp.where(lane >= L_LANE, 1.0, 0.0)
        for half in range(2):
            tile = jnp.where(lane == 2 * g + half, m[_block_rows(half)], tile)
            tile = jnp.where(lane == L_LANE + 2 * g + half, r[_block_rows(half), LANES:], tile)
        if g == A_GROUPS - 1:
            st_ref[sq, _block_rows(jb), :] = tile
        else:
            tiles[(sq, jb)] = tile

    _pipeline(units, scores_of, probs_of, finish)


def _dilated_attn(q, k, v):
    n_seq, length, _ = q.shape
    blocks = min(ATTN_BLOCKS, length // BLOCK)
    seqs = ATTN_BLOCKS // blocks
    cur = pl.BlockSpec((seqs, blocks * BLOCK, A_WIDTH), lambda s, i: (s, i, 0))
    prev = pl.BlockSpec((seqs, BLOCK, A_WIDTH), lambda s, i: (s, jnp.maximum(i * blocks - 1, 0), 0))
    stat = pl.BlockSpec((seqs, blocks * BLOCK, LANES), lambda s, i: (s, i, 0))
    return pl.pallas_call(
        functools.partial(_dilated_attn_kernel, seqs=seqs, blocks=blocks),
        grid=(n_seq // seqs, length // (blocks * BLOCK)),
        in_specs=[pl.BlockSpec((2, 2 * BLOCK, 2 * BLOCK), lambda s, i: (0, 0, 0)),
                  cur, prev, cur, prev, cur],
        out_specs=[cur, stat],
        out_shape=[jax.ShapeDtypeStruct((n_seq, length, A_WIDTH), BF16),
                   jax.ShapeDtypeStruct((n_seq, length, LANES), F32)],
        compiler_params=pltpu.CompilerParams(
            dimension_semantics=("parallel", "arbitrary"), vmem_limit_bytes=VMEM_LIMIT),
        name="dilated_attn",
    )(_band_bias(A_MAX_DIST), q, k, k, v, v)


def _expand_heads(w):
    head_of_col = lax.broadcasted_iota(jnp.int32, (LANES, A_WIDTH), 1) // HEAD_DIM
    lane = lax.broadcasted_iota(jnp.int32, (LANES, A_WIDTH), 0)
    expand = (head_of_col == lane).astype(BF16)
    hi = w.astype(BF16)
    lo = (w - hi.astype(F32)).astype(BF16)
    return _dot(hi, expand) + _dot(lo, expand)


def _gated_conv(gb_ref, u_ref, u_tail_ref, cw_ref, part):
    u_cur = u_ref[0, part, :].astype(F32)
    if part.start == 0:
        u_prev = jnp.where(pl.program_id(1) > 0, u_tail_ref[0].astype(F32), 0.0)
    else:
        u_prev = u_ref[0, part.start - CONV_TAIL:part.start, :].astype(F32)
    head_rows = lax.broadcasted_iota(jnp.int32, (SUBLANES, CONV_CH), 0)
    acc = cw_ref[CONV_K - 1:CONV_K, :] * u_cur
    for shift in range(1, CONV_K):
        shifted = pltpu.roll(u_cur, shift, 0)
        from_prev = pltpu.roll(u_prev, shift, 0)[:SUBLANES]
        head = jnp.where(head_rows < shift, from_prev, shifted[:SUBLANES])
        shifted = jnp.concatenate([head, shifted[SUBLANES:]], axis=0)
        acc = acc + cw_ref[CONV_K - 1 - shift:CONV_K - shift, :] * shifted
    return gb_ref[0, part, :].astype(F32) * acc


def _mix_mlp_kernel(x_ref, o1_ref, o2_ref, o3_ref, s1_ref, s2_ref, s3_ref,
                    gb_ref, u_ref, u_tail_ref, yc_ref, cw_ref, gg_a_ref, gg_b_ref, wo_ref, g_mlp_ref, w1_ref, w2_ref,
                    g_next_ref, *refs, final):
    if final:
        out_ref, o_nat, s_nat, tmp_nat = refs
    else:
        out_ref, h_ref, o_nat, s_nat, tmp_nat = refs
    tm = x_ref.shape[1]
    quarter = tm // 4

    def to_token_order(dst, tmp, src4, src16):
        for b in range(4):
            dst[0][pl.ds(b, quarter, stride=4), :] = src4(b)
            for a in range(4):
                tmp[pl.ds(b * quarter + a, quarter // 4, stride=4), :] = src16(4 * a + b)
        for b in range(4):
            dst[1][pl.ds(b, quarter, stride=4), :] = tmp[b * quarter:(b + 1) * quarter, :]

    to_token_order((s_nat.at[0], s_nat.at[1]), tmp_nat, lambda r: s2_ref[0, r], lambda r: s3_ref[0, r])
    for g in range(A_GROUPS):
        to_token_order((o_nat.at[g], o_nat.at[A_GROUPS + g]), tmp_nat,
                       lambda r: o2_ref[0, r, :, _lane_group(g)].astype(F32),
                       lambda r: o3_ref[0, r, :, _lane_group(g)].astype(F32))

    parts = [slice(i * MLP_PART, (i + 1) * MLP_PART) for i in range(tm // MLP_PART)]

    def mixed_heads(part):
        outs = [o1_ref[0, part, :].astype(F32)] + [
            jnp.concatenate([o_nat[j * A_GROUPS + g, part, :] for g in range(A_GROUPS)], axis=1) for j in range(2)]
        tops = (s1_ref[0, part, :], s_nat[0, part, :], s_nat[1, part, :])
        sums = [pltpu.roll(t, LANES - L_LANE, 1) for t in tops]
        top = jnp.maximum(jnp.maximum(tops[0], tops[1]), tops[2])
        es = [jnp.exp2(m - top) for m in tops]
        inv = 1.0 / (es[0] * sums[0] + es[1] * sums[1] + es[2] * sums[2])
        is_head = lax.broadcasted_iota(jnp.int32, (MLP_PART, LANES), 1) < A_HEADS
        ya = jnp.zeros((MLP_PART, A_WIDTH), F32)
        for e, o in zip(es, outs):
            ya = ya + _expand_heads(jnp.where(is_head, e * inv, 0.0)) * o
        return (_rms_normalize(ya) * gg_a_ref[...]).astype(BF16)

    ya = [mixed_heads(part) for part in parts]
    yb = [(_rms_normalize(_gated_conv(gb_ref, u_ref, u_tail_ref, cw_ref, part)) * gg_b_ref[...]).astype(BF16)
          for part in parts]
    xs = [x_ref[0, part, :] + _dot(jnp.concatenate([a, b, yc_ref[0, part, :]], axis=1), wo_ref[...])
          for part, a, b in zip(parts, ya, yb)]
    hs = [(_rms_normalize(x) * g_mlp_ref[...]).astype(BF16) for x in xs]
    ffs = [jnp.zeros((MLP_PART, D_MODEL), F32) for _ in parts]
    for c in range(D_FF // FF_CHUNK):
        cols = slice(c * FF_CHUNK, (c + 1) * FF_CHUNK)
        for i, h in enumerate(hs):
            a = jnp.square(jnp.maximum(_dot(h, w1_ref[:, cols]), 0.0))
            ffs[i] = ffs[i] + _dot(a.astype(BF16), w2_ref[cols, :])
    for part, x, ff in zip(parts, xs, ffs):
        x = x + ff
        if final:
            out_ref[0, part, :] = _rms_normalize(x) * g_next_ref[...]
        else:
            out_ref[0, part, :] = x
            h_ref[0, part, :] = (_rms_normalize(x) * g_next_ref[...]).astype(BF16)


def _mix_mlp(x, outs, stats, gb, u, yc, conv_w, gg_a, gg_b, wo, g_mlp, w1, w2, g_next, final):
    batch, seq, _ = x.shape
    tm = MLP_TOKENS

    def rows(width):
        return pl.BlockSpec((1, tm, width), lambda b, i: (b, i, 0))

    def sub(dil, width):
        return pl.BlockSpec((1, dil, tm // dil, width), lambda b, i: (b, 0, i, 0))

    def const(shape):
        return pl.BlockSpec(shape, lambda b, i: (0, 0), pipeline_mode=pl.Buffered(1))

    tail = pl.BlockSpec((1, CONV_TAIL, CONV_CH), lambda b, i: (b, jnp.maximum(i * (tm // CONV_TAIL) - 1, 0), 0))
    return pl.pallas_call(
        functools.partial(_mix_mlp_kernel, final=final),
        grid=(batch, seq // tm),
        in_specs=[
            rows(D_MODEL), rows(A_WIDTH), sub(DILATIONS[1], A_WIDTH), sub(DILATIONS[2], A_WIDTH),
            rows(LANES), sub(DILATIONS[1], LANES), sub(DILATIONS[2], LANES),
            rows(CONV_CH), rows(CONV_CH), tail, rows(C_WIDTH),
            const((CONV_K, CONV_CH)), const((1, A_WIDTH)), const((1, CONV_CH)), const((MIX_WIDTH, D_MODEL)),
            const((1, D_MODEL)),
            const((D_MODEL, D_FF)), const((D_FF, D_MODEL)), const((1, D_MODEL)),
        ],
        out_specs=[rows(D_MODEL)] * (1 if final else 2),
        out_shape=[jax.ShapeDtypeStruct((batch, seq, D_MODEL), F32)] + (
            [] if final else [jax.ShapeDtypeStruct((batch, seq, D_MODEL), BF16)]),
        scratch_shapes=[pltpu.VMEM((2 * A_GROUPS, tm, LANES), F32), pltpu.VMEM((2, tm, LANES), F32),
                        pltpu.VMEM((tm, LANES), F32)],
        compiler_params=pltpu.CompilerParams(
            dimension_semantics=("parallel", "parallel"), vmem_limit_bytes=VMEM_LIMIT),
        name="mix_mlp",
    )(x, *outs, *stats, gb, u, u, yc, conv_w, gg_a, gg_b, wo, g_mlp, w1, w2, g_next)


def kernel(x, w_in, conv_w, sinks, g_mix, g_group, w_o, g_mlp, w_ff_in, w_ff_out, g_final):
    batch, seq, _ = x.shape
    depth = w_in.shape[0]
    qc_start = sum(IN_SPLITS[:6])
    c_start = A_WIDTH + CONV_CH
    q_scale = np.ones((IN_WIDTH,), np.float32)
    q_scale[0:A_WIDTH] = HEAD_DIM ** -0.5 * LOG2E
    q_scale[qc_start:qc_start + C_WIDTH] = HEAD_DIM ** -0.5 * LOG2E

    def reorder_c_heads(t, axis):
        heads = [lax.slice_in_dim(t, h * HEAD_DIM, (h + 1) * HEAD_DIM, axis=axis) for h in C_HEAD_ORDER]
        return jnp.concatenate(heads, axis=axis)

    def project_columns(w):
        w = w * q_scale
        w = jnp.concatenate([w[:, :qc_start], reorder_c_heads(w[:, qc_start:qc_start + C_WIDTH], 1),
                             w[:, qc_start + C_WIDTH:]], axis=1)
        groups = [lax.slice_in_dim(w, IN_START[name] + g * LANES, IN_START[name] + (g + 1) * LANES, axis=1)
                  for name, g in PROJ_GROUPS]
        return jnp.concatenate(groups, axis=1).astype(BF16)

    h = None
    for l in range(depth):
        final = l == depth - 1
        gg = g_group[l]
        (o1, st1, yc, q4, k4, v4, q16, k16, v16, gb, u) = _proj_attn(
            x if h is None else h, g_mix[l].reshape(1, D_MODEL), project_columns(w_in[l]),
            sinks[l].reshape(C_Q_HEADS), reorder_c_heads(gg[c_start:], 0).reshape(1, C_WIDTH),
            normalized=h is not None)

        outs, stats = [o1], [st1]
        for dil, (q, k, v) in zip(DILATIONS[1:], ((q4, k4, v4), (q16, k16, v16))):
            o, st = _dilated_attn(*(t.reshape(batch * dil, seq // dil, A_WIDTH) for t in (q, k, v)))
            outs.append(o.reshape(batch, dil, seq // dil, A_WIDTH))
            stats.append(st.reshape(batch, dil, seq // dil, LANES))

        w_o_l = jnp.concatenate([w_o[l][:c_start], reorder_c_heads(w_o[l][c_start:], 0)], axis=0).astype(BF16)
        g_next = g_final if final else g_mix[l + 1]
        res = _mix_mlp(x, outs, stats, gb, u, yc, conv_w[l], gg[:A_WIDTH].reshape(1, A_WIDTH),
                       gg[A_WIDTH:c_start].reshape(1, CONV_CH), w_o_l,
                       g_mlp[l].reshape(1, D_MODEL), w_ff_in[l].astype(BF16), w_ff_out[l].astype(BF16),
                       g_next.reshape(1, D_MODEL), final=final)
        x = res[0]
        h = None if final else res[1]
    return x
```

```python
import functools

import jax
import jax.numpy as jnp
import numpy as np
from jax import lax
from jax.experimental import pallas as pl
from jax.experimental.pallas import tpu as pltpu

D_MODEL = 1024
HEAD_DIM = 64
A_HEADS = 6
DILATIONS = (1, 4, 16)
A_MAX_DIST = 128
CONV_CH = 256
CONV_K = 3
C_Q_HEADS = 6
C_KV_HEADS = 2
C_GROUP = C_Q_HEADS // C_KV_HEADS
C_MAX_DIST = 127
BLOCK = 128
D_FF = 4 * D_MODEL
EPS = 1e-6

A_WIDTH = A_HEADS * HEAD_DIM
C_WIDTH = C_Q_HEADS * HEAD_DIM
KV_WIDTH = C_KV_HEADS * HEAD_DIM
MIX_WIDTH = A_WIDTH + CONV_CH + C_WIDTH
IN_NAMES = ("qa", "ka", "va", "gb", "gc", "xb", "qc", "kc", "vc")
IN_SPLITS = (A_WIDTH, A_WIDTH, A_WIDTH, CONV_CH, CONV_CH, CONV_CH, C_WIDTH, KV_WIDTH, KV_WIDTH)
IN_WIDTH = sum(IN_SPLITS)
IN_START = dict(zip(IN_NAMES, np.cumsum((0,) + IN_SPLITS[:-1]).tolist()))
LANES = 128
SUBLANES = 8
A_GROUPS = A_WIDTH // LANES
CONV_TAIL = 16
C_HEAD_ORDER = (0, 3, 1, 4, 2, 5)
ATTN_BLOCKS = 16
LOG2E = 1.4426950408889634
L_LANE = 64

PROJ_TOKENS = 1024
PROJ_PART = 256
PROJ_CHUNK = 512
PROJ_GROUPS = ([(name, g) for name in ("qa", "ka", "va") for g in range(3)]
               + [("gb", 0), ("gb", 1), ("qc", 0), ("gc", 0), ("xb", 0), ("gc", 1), ("xb", 1),
                  ("qc", 1), ("qc", 2), ("kc", 0), ("vc", 0)])
MLP_TOKENS = 512
MLP_PART = 256
FF_CHUNK = 1024
VMEM_LIMIT = 56 * 1024 * 1024

NEG = -1e30
BF16 = jnp.bfloat16
F32 = jnp.float32


def _rms_normalize(t):
    return t * lax.rsqrt(jnp.mean(t * t, axis=-1, keepdims=True) + EPS)


def _dot(a, b):
    return jnp.dot(a, b, preferred_element_type=F32)


def _dot_nt(a, b):
    return lax.dot_general(a, b, (((1,), (1,)), ((), ())), preferred_element_type=F32)


def _lane_group(g):
    return slice(g * LANES, (g + 1) * LANES)


LOCAL_NAMES = ("qa", "ka", "va", "qc", "kc", "vc")
LOCAL_WIDTHS = (A_WIDTH, A_WIDTH, A_WIDTH, C_WIDTH, KV_WIDTH, KV_WIDTH)
CARRY_NAMES = ("ka", "va", "kc", "vc")
CARRY_WIDTHS = (A_WIDTH, A_WIDTH, KV_WIDTH, KV_WIDTH)


def _proj_attn_kernel(sink_ref, x_ref, g_ref, w_ref, bias_a_ref, bias_c_ref, perm_ref, gg_c_ref, *refs, normalized):
    o1_ref, st1_ref, yc_ref = refs[0:3]
    strided = (refs[3:6], refs[6:9])
    gb_ref, u_ref = refs[9:11]
    h_ref = refs[11]
    local = dict(zip(LOCAL_NAMES, refs[12:18]))
    carry = dict(zip(CARRY_NAMES, refs[18:22]))
    tm = x_ref.shape[1]
    blocks = tm // BLOCK

    @pl.when(pl.program_id(1) == 0)
    def _():
        for ref in carry.values():
            ref[...] = jnp.zeros(ref.shape, ref.dtype)

    parts = [slice(i * PROJ_PART, (i + 1) * PROJ_PART) for i in range(tm // PROJ_PART)]

    if not normalized:
        for part in parts:
            h_ref[part, :] = (_rms_normalize(x_ref[0, part, :]) * g_ref[...]).astype(BF16)

    def lhs(part):
        return x_ref[0, part, :] if normalized else h_ref[part, :]

    per_chunk = PROJ_CHUNK // LANES
    for c in range(IN_WIDTH // PROJ_CHUNK):
        for part in parts:
            z = _dot(lhs(part), w_ref[:, c * PROJ_CHUNK:(c + 1) * PROJ_CHUNK])
            gates = {}
            for k in range(per_chunk):
                name, g = PROJ_GROUPS[c * per_chunk + k]
                zk = z[:, _lane_group(k)]
                if name == "gc":
                    gates[g] = zk
                elif name == "xb":
                    u_ref[0, part, _lane_group(g)] = (gates.pop(g) * zk).astype(BF16)
                elif name == "gb":
                    gb_ref[0, part, _lane_group(g)] = zk.astype(BF16)
                else:
                    local[name][0, part, _lane_group(g)] = zk.astype(BF16)

    for i, part in enumerate(parts):
        qkv = jnp.concatenate([local[name][0, part, :] for name in ("qa", "ka", "va")], axis=1)
        for d, dil in enumerate(DILATIONS[1:]):
            n = PROJ_PART // dil
            moved = _dot(perm_ref[d], qkv).astype(BF16)
            for r in range(dil):
                for t in range(3):
                    strided[d][t][0, r, i * n:(i + 1) * n, :] = moved[r * n:(r + 1) * n, t * A_WIDTH:(t + 1) * A_WIDTH]

    lane = lax.broadcasted_iota(jnp.int32, (BLOCK, LANES), 1)
    first_bias = jnp.minimum(pl.program_id(1), 1)
    upper = lax.broadcasted_iota(jnp.int32, (2 * BLOCK, 1), 0) < BLOCK
    sinks = [jnp.where(upper, sink_ref[C_HEAD_ORDER[2 * g]], sink_ref[C_HEAD_ORDER[2 * g + 1]]) * LOG2E
             for g in range(C_WIDTH // LANES)]
    kv_lanes = slice(0, KV_WIDTH)
    units = [(kind, jb, g) for jb in range(blocks) for kind in ("a", "c") for g in range(A_GROUPS)]
    tiles, c_parts = {}, {}

    def scores_of(unit):
        kind, jb, g = unit
        bias_ref = bias_a_ref if kind == "a" else bias_c_ref
        bias = bias_ref[first_bias] if jb == 0 else bias_ref[1]
        if kind == "a":
            keys = _keys_of_block(carry["ka"], local["ka"], 0, jb, _lane_group(g))
            return _pair_scores(local["qa"][0, _block_rows(jb), _lane_group(g)], keys, bias)
        keys = _keys_of_block(carry["kc"], local["kc"], 0, jb, kv_lanes)
        return _pair_scores(local["qc"][0, _block_rows(jb), _lane_group(g)], keys, bias)

    def probs_of(unit, s):
        kind, _, g = unit
        return _pair_probs(s, None if kind == "a" else sinks[g])

    def finish(unit, probs):
        kind, jb, g = unit
        p, m = probs
        if kind == "a":
            r = _pair_values(p, _keys_of_block(carry["va"], local["va"], 0, jb, _lane_group(g)))
            o1_ref[0, _block_rows(jb), _lane_group(g)] = _both_heads(r[:, :LANES]).astype(o1_ref.dtype)
            tile = tiles.pop(jb, None)
            if tile is None:
                tile = jnp.where(lane >= L_LANE, 1.0, 0.0)
            for half in range(2):
                tile = jnp.where(lane == 2 * g + half, m[_block_rows(half)], tile)
                tile = jnp.where(lane == L_LANE + 2 * g + half, r[_block_rows(half), LANES:], tile)
            if g == A_GROUPS - 1:
                st1_ref[0, _block_rows(jb), :] = tile
            else:
                tiles[jb] = tile
        else:
            r = _pair_values(p, _keys_of_block(carry["vc"], local["vc"], 0, jb, kv_lanes))
            denom = r[:, LANES:] + jnp.exp2(sinks[g] - m)
            c_parts.setdefault(jb, []).append(_both_heads(r[:, :LANES]) / _both_heads(denom))
            if g == A_GROUPS - 1:
                oc = jnp.concatenate(c_parts.pop(jb), axis=1)
                yc_ref[0, _block_rows(jb), :] = (_rms_normalize(oc) * gg_c_ref[...]).astype(yc_ref.dtype)

    _pipeline(units, scores_of, probs_of, finish)

    for name in CARRY_NAMES:
        carry[name][0] = local[name][0, _block_rows(blocks - 1), :]


def _residue_permutations():
    mats = np.zeros((len(DILATIONS) - 1, PROJ_PART, PROJ_PART), np.float32)
    for d, dil in enumerate(DILATIONS[1:]):
        n = PROJ_PART // dil
        for r in range(dil):
            mats[d, r * n + np.arange(n), dil * np.arange(n) + r] = 1.0
    return jnp.asarray(mats, BF16)


def _proj_attn(x, g, w, sinks, gg_c, normalized):
    batch, seq, _ = x.shape
    tm = PROJ_TOKENS

    def rows(width):
        return pl.BlockSpec((1, tm, width), lambda b, i: (b, i, 0))

    def sub(dil):
        return pl.BlockSpec((1, dil, tm // dil, A_WIDTH), lambda b, i: (b, 0, i, 0))

    def const(shape):
        return pl.BlockSpec(shape, lambda b, i: (0,) * len(shape), pipeline_mode=pl.Buffered(1))

    out_specs = [rows(A_WIDTH), rows(LANES), rows(C_WIDTH)]
    out_shape = [jax.ShapeDtypeStruct((batch, seq, A_WIDTH), BF16), jax.ShapeDtypeStruct((batch, seq, LANES), F32),
                 jax.ShapeDtypeStruct((batch, seq, C_WIDTH), BF16)]
    for dil in DILATIONS[1:]:
        out_specs += [sub(dil)] * 3
        out_shape += [jax.ShapeDtypeStruct((batch, dil, seq // dil, A_WIDTH), BF16)] * 3
    out_specs += [rows(CONV_CH)] * 2
    out_shape += [jax.ShapeDtypeStruct((batch, seq, CONV_CH), BF16)] * 2
    scratch = ([pltpu.VMEM((tm, D_MODEL), BF16)]
               + [pltpu.VMEM((1, tm, wd), BF16) for wd in LOCAL_WIDTHS]
               + [pltpu.VMEM((1, BLOCK, wd), BF16) for wd in CARRY_WIDTHS])
    bias_spec = const((2, 2 * BLOCK, 2 * BLOCK))
    return pl.pallas_call(
        functools.partial(_proj_attn_kernel, normalized=normalized),
        grid=(batch, seq // tm),
        in_specs=[pl.BlockSpec(memory_space=pltpu.SMEM), rows(D_MODEL), const((1, D_MODEL)),
                  const((D_MODEL, IN_WIDTH)), bias_spec, bias_spec,
                  const((len(DILATIONS) - 1, PROJ_PART, PROJ_PART)), const((1, C_WIDTH))],
        out_specs=out_specs,
        out_shape=out_shape,
        scratch_shapes=scratch,
        compiler_params=pltpu.CompilerParams(
            dimension_semantics=("parallel", "arbitrary"), vmem_limit_bytes=VMEM_LIMIT),
        name="proj_attn",
    )(sinks, x, g, w, _band_bias(A_MAX_DIST), _band_bias(C_MAX_DIST), _residue_permutations(), gg_c)


def _band_bias(max_dist):
    qi = np.arange(2 * BLOCK)[:, None] % BLOCK
    kj = np.arange(2 * BLOCK)[None, :]
    dist = BLOCK + qi - kj
    band = (dist >= 0) & (dist <= max_dist)
    first = band & (kj >= BLOCK)
    return jnp.asarray(np.where(np.stack([first, band]), 0.0, NEG), F32)


def _low_half():
    return lax.broadcasted_iota(jnp.int32, (BLOCK, LANES), 1) < HEAD_DIM


def _pair_scores(q2, k2, bias):
    low = _low_half()
    zero = jnp.zeros_like(q2)
    stacked = jnp.concatenate([jnp.where(low, q2, zero), jnp.where(low, zero, q2)], axis=0)
    return _dot_nt(stacked, k2) + bias


def _pair_probs(s, sink=None):
    m = jnp.max(s, axis=-1, keepdims=True)
    if sink is not None:
        m = jnp.maximum(m, sink)
    return jnp.exp2(s - m).astype(BF16), m


def _pair_values(p, v2):
    ones = jnp.ones((2 * BLOCK, LANES), BF16)
    return _dot(p, jnp.concatenate([v2, ones], axis=1))


def _both_heads(t):
    return jnp.where(_low_half(), t[:BLOCK], t[BLOCK:])


def _pipeline(units, scores_of, probs_of, finish):
    n = len(units)
    scores, probs = {}, {}
    for t in range(n + 2):
        if t >= 2:
            finish(units[t - 2], probs.pop(t - 2))
        if 1 <= t <= n:
            probs[t - 1] = probs_of(units[t - 1], scores.pop(t - 1))
        if t < n:
            scores[t] = scores_of(units[t])


def _block_rows(jb):
    return slice(jb * BLOCK, (jb + 1) * BLOCK)


def _keys_of_block(prev_ref, cur_ref, sq, jb, cols):
    before = prev_ref[sq, :, cols] if jb == 0 else cur_ref[sq, _block_rows(jb - 1), cols]
    return jnp.concatenate([before, cur_ref[sq, _block_rows(jb), cols]], axis=0)


def _dilated_attn_kernel(bias_ref, q_ref, kp_ref, kc_ref, vp_ref, vc_ref, o_ref, st_ref, *, seqs, blocks):
    lane = lax.broadcasted_iota(jnp.int32, (BLOCK, LANES), 1)
    first_bias = jnp.minimum(pl.program_id(1), 1)
    units = [(sq, jb, g) for sq in range(seqs) for jb in range(blocks) for g in range(A_GROUPS)]
    tiles = {}

    def scores_of(unit):
        sq, jb, g = unit
        bias = bias_ref[first_bias] if jb == 0 else bias_ref[1]
        return _pair_scores(q_ref[sq, _block_rows(jb), _lane_group(g)],
                            _keys_of_block(kp_ref, kc_ref, sq, jb, _lane_group(g)), bias)

    def probs_of(unit, s):
        return _pair_probs(s)

    def finish(unit, probs):
        sq, jb, g = unit
        p, m = probs
        r = _pair_values(p, _keys_of_block(vp_ref, vc_ref, sq, jb, _lane_group(g)))
        o_ref[sq, _block_rows(jb), _lane_group(g)] = _both_heads(r[:, :LANES]).astype(o_ref.dtype)
        tile = tiles.pop((sq, jb), None)
        if tile is None:
            tile = jnp.where(lane >= L_LANE, 1.0, 0.0)
        for half in range(2):
            tile = jnp.where(lane == 2 * g + half, m[_block_rows(half)], tile)
            tile = jnp.where(lane == L_LANE + 2 * g + half, r[_block_rows(half), LANES:], tile)
        if g == A_GROUPS - 1:
            st_ref[sq, _block_rows(jb), :] = tile
        else:
            tiles[(sq, jb)] = tile

    _pipeline(units, scores_of, probs_of, finish)


def _dilated_attn(q, k, v):
    n_seq, length, _ = q.shape
    blocks = min(ATTN_BLOCKS, length // BLOCK)
    seqs = ATTN_BLOCKS // blocks
    cur = pl.BlockSpec((seqs, blocks * BLOCK, A_WIDTH), lambda s, i: (s, i, 0))
    prev = pl.BlockSpec((seqs, BLOCK, A_WIDTH), lambda s, i: (s, jnp.maximum(i * blocks - 1, 0), 0))
    stat = pl.BlockSpec((seqs, blocks * BLOCK, LANES), lambda s, i: (s, i, 0))
    return pl.pallas_call(
        functools.partial(_dilated_attn_kernel, seqs=seqs, blocks=blocks),
        grid=(n_seq // seqs, length // (blocks * BLOCK)),
        in_specs=[pl.BlockSpec((2, 2 * BLOCK, 2 * BLOCK), lambda s, i: (0, 0, 0)),
                  cur, prev, cur, prev, cur],
        out_specs=[cur, stat],
        out_shape=[jax.ShapeDtypeStruct((n_seq, length, A_WIDTH), BF16),
                   jax.ShapeDtypeStruct((n_seq, length, LANES), F32)],
        compiler_params=pltpu.CompilerParams(
            dimension_semantics=("parallel", "arbitrary"), vmem_limit_bytes=VMEM_LIMIT),
        name="dilated_attn",
    )(_band_bias(A_MAX_DIST), q, k, k, v, v)


def _expand_heads(w):
    head_of_col = lax.broadcasted_iota(jnp.int32, (LANES, A_WIDTH), 1) // HEAD_DIM
    lane = lax.broadcasted_iota(jnp.int32, (LANES, A_WIDTH), 0)
    return _dot(w.astype(BF16), (head_of_col == lane).astype(BF16))


def _gated_conv(gb_ref, u_ref, u_tail_ref, cw_ref, part):
    u_cur = u_ref[0, part, :].astype(F32)
    if part.start == 0:
        u_prev = jnp.where(pl.program_id(1) > 0, u_tail_ref[0].astype(F32), 0.0)
    else:
        u_prev = u_ref[0, part.start - CONV_TAIL:part.start, :].astype(F32)
    head_rows = lax.broadcasted_iota(jnp.int32, (SUBLANES, CONV_CH), 0)
    acc = cw_ref[CONV_K - 1:CONV_K, :] * u_cur
    for shift in range(1, CONV_K):
        shifted = pltpu.roll(u_cur, shift, 0)
        from_prev = pltpu.roll(u_prev, shift, 0)[:SUBLANES]
        head = jnp.where(head_rows < shift, from_prev, shifted[:SUBLANES])
        shifted = jnp.concatenate([head, shifted[SUBLANES:]], axis=0)
        acc = acc + cw_ref[CONV_K - 1 - shift:CONV_K - shift, :] * shifted
    return gb_ref[0, part, :].astype(F32) * acc


def _mix_mlp_kernel(x_ref, o1_ref, o2_ref, o3_ref, s1_ref, s2_ref, s3_ref,
                    gb_ref, u_ref, u_tail_ref, yc_ref, cw_ref, gg_a_ref, gg_b_ref, wo_ref, g_mlp_ref, w1_ref, w2_ref,
                    g_next_ref, *refs, final):
    if final:
        out_ref, o_nat, s_nat, tmp_nat = refs
    else:
        out_ref, h_ref, o_nat, s_nat, tmp_nat = refs
    tm = x_ref.shape[1]
    quarter = tm // 4

    def to_token_order(dst, tmp, src4, src16):
        for b in range(4):
            dst[0][pl.ds(b, quarter, stride=4), :] = src4(b)
            for a in range(4):
                tmp[pl.ds(b * quarter + a, quarter // 4, stride=4), :] = src16(4 * a + b)
        for b in range(4):
            dst[1][pl.ds(b, quarter, stride=4), :] = tmp[b * quarter:(b + 1) * quarter, :]

    to_token_order((s_nat.at[0], s_nat.at[1]), tmp_nat, lambda r: s2_ref[0, r], lambda r: s3_ref[0, r])
    for g in range(A_GROUPS):
        to_token_order((o_nat.at[g], o_nat.at[A_GROUPS + g]), tmp_nat,
                       lambda r: o2_ref[0, r, :, _lane_group(g)].astype(F32),
                       lambda r: o3_ref[0, r, :, _lane_group(g)].astype(F32))

    parts = [slice(i * MLP_PART, (i + 1) * MLP_PART) for i in range(tm // MLP_PART)]

    def mixed_heads(part):
        outs = [o1_ref[0, part, :].astype(F32)] + [
            jnp.concatenate([o_nat[j * A_GROUPS + g, part, :] for g in range(A_GROUPS)], axis=1) for j in range(2)]
        tops = (s1_ref[0, part, :], s_nat[0, part, :], s_nat[1, part, :])
        sums = [pltpu.roll(t, LANES - L_LANE, 1) for t in tops]
        top = jnp.maximum(jnp.maximum(tops[0], tops[1]), tops[2])
        es = [jnp.exp2(m - top) for m in tops]
        inv = 1.0 / (es[0] * sums[0] + es[1] * sums[1] + es[2] * sums[2])
        is_head = lax.broadcasted_iota(jnp.int32, (MLP_PART, LANES), 1) < A_HEADS
        ya = jnp.zeros((MLP_PART, A_WIDTH), F32)
        for e, o in zip(es, outs):
            ya = ya + _expand_heads(jnp.where(is_head, e * inv, 0.0)) * o
        return (_rms_normalize(ya) * gg_a_ref[...]).astype(BF16)

    ya = [mixed_heads(part) for part in parts]
    yb = [(_rms_normalize(_gated_conv(gb_ref, u_ref, u_tail_ref, cw_ref, part)) * gg_b_ref[...]).astype(BF16)
          for part in parts]
    xs = [x_ref[0, part, :] + _dot(jnp.concatenate([a, b, yc_ref[0, part, :]], axis=1), wo_ref[...])
          for part, a, b in zip(parts, ya, yb)]
    hs = [(_rms_normalize(x) * g_mlp_ref[...]).astype(BF16) for x in xs]
    ffs = [jnp.zeros((MLP_PART, D_MODEL), F32) for _ in parts]
    for c in range(D_FF // FF_CHUNK):
        cols = slice(c * FF_CHUNK, (c + 1) * FF_CHUNK)
        for i, h in enumerate(hs):
            a = jnp.square(jnp.maximum(_dot(h, w1_ref[:, cols]), 0.0))
            ffs[i] = ffs[i] + _dot(a.astype(BF16), w2_ref[cols, :])
    for part, x, ff in zip(parts, xs, ffs):
        x = x + ff
        if final:
            out_ref[0, part, :] = _rms_normalize(x) * g_next_ref[...]
        else:
            out_ref[0, part, :] = x
            h_ref[0, part, :] = (_rms_normalize(x) * g_next_ref[...]).astype(BF16)


def _mix_mlp(x, outs, stats, gb, u, yc, conv_w, gg_a, gg_b, wo, g_mlp, w1, w2, g_next, final):
    batch, seq, _ = x.shape
    tm = MLP_TOKENS

    def rows(width):
        return pl.BlockSpec((1, tm, width), lambda b, i: (b, i, 0))

    def sub(dil, width):
        return pl.BlockSpec((1, dil, tm // dil, width), lambda b, i: (b, 0, i, 0))

    def const(shape):
        return pl.BlockSpec(shape, lambda b, i: (0, 0), pipeline_mode=pl.Buffered(1))

    tail = pl.BlockSpec((1, CONV_TAIL, CONV_CH), lambda b, i: (b, jnp.maximum(i * (tm // CONV_TAIL) - 1, 0), 0))
    return pl.pallas_call(
        functools.partial(_mix_mlp_kernel, final=final),
        grid=(batch, seq // tm),
        in_specs=[
            rows(D_MODEL), rows(A_WIDTH), sub(DILATIONS[1], A_WIDTH), sub(DILATIONS[2], A_WIDTH),
            rows(LANES), sub(DILATIONS[1], LANES), sub(DILATIONS[2], LANES),
            rows(CONV_CH), rows(CONV_CH), tail, rows(C_WIDTH),
            const((CONV_K, CONV_CH)), const((1, A_WIDTH)), const((1, CONV_CH)), const((MIX_WIDTH, D_MODEL)),
            const((1, D_MODEL)),
            const((D_MODEL, D_FF)), const((D_FF, D_MODEL)), const((1, D_MODEL)),
        ],
        out_specs=[rows(D_MODEL)] * (1 if final else 2),
        out_shape=[jax.ShapeDtypeStruct((batch, seq, D_MODEL), F32)] + (
            [] if final else [jax.ShapeDtypeStruct((batch, seq, D_MODEL), BF16)]),
        scratch_shapes=[pltpu.VMEM((2 * A_GROUPS, tm, LANES), F32), pltpu.VMEM((2, tm, LANES), F32),
                        pltpu.VMEM((tm, LANES), F32)],
        compiler_params=pltpu.CompilerParams(
            dimension_semantics=("parallel", "parallel"), vmem_limit_bytes=VMEM_LIMIT),
        name="mix_mlp",
    )(x, *outs, *stats, gb, u, u, yc, conv_w, gg_a, gg_b, wo, g_mlp, w1, w2, g_next)


def kernel(x, w_in, conv_w, sinks, g_mix, g_group, w_o, g_mlp, w_ff_in, w_ff_out, g_final):
    batch, seq, _ = x.shape
    depth = w_in.shape[0]
    qc_start = sum(IN_SPLITS[:6])
    c_start = A_WIDTH + CONV_CH
    q_scale = np.ones((IN_WIDTH,), np.float32)
    q_scale[0:A_WIDTH] = HEAD_DIM ** -0.5 * LOG2E
    q_scale[qc_start:qc_start + C_WIDTH] = HEAD_DIM ** -0.5 * LOG2E

    def reorder_c_heads(t, axis):
        heads = [lax.slice_in_dim(t, h * HEAD_DIM, (h + 1) * HEAD_DIM, axis=axis) for h in C_HEAD_ORDER]
        return jnp.concatenate(heads, axis=axis)

    def project_columns(w):
        w = w * q_scale
        w = jnp.concatenate([w[:, :qc_start], reorder_c_heads(w[:, qc_start:qc_start + C_WIDTH], 1),
                             w[:, qc_start + C_WIDTH:]], axis=1)
        groups = [lax.slice_in_dim(w, IN_START[name] + g * LANES, IN_START[name] + (g + 1) * LANES, axis=1)
                  for name, g in PROJ_GROUPS]
        return jnp.concatenate(groups, axis=1).astype(BF16)

    h = None
    for l in range(depth):
        final = l == depth - 1
        gg = g_group[l]
        (o1, st1, yc, q4, k4, v4, q16, k16, v16, gb, u) = _proj_attn(
            x if h is None else h, g_mix[l].reshape(1, D_MODEL), project_columns(w_in[l]),
            sinks[l].reshape(C_Q_HEADS), reorder_c_heads(gg[c_start:], 0).reshape(1, C_WIDTH),
            normalized=h is not None)

        outs, stats = [o1], [st1]
        for dil, (q, k, v) in zip(DILATIONS[1:], ((q4, k4, v4), (q16, k16, v16))):
            o, st = _dilated_attn(*(t.reshape(batch * dil, seq // dil, A_WIDTH) for t in (q, k, v)))
            outs.append(o.reshape(batch, dil, seq // dil, A_WIDTH))
            stats.append(st.reshape(batch, dil, seq // dil, LANES))

        w_o_l = jnp.concatenate([w_o[l][:c_start], reorder_c_heads(w_o[l][c_start:], 0)], axis=0).astype(BF16)
        g_next = g_final if final else g_mix[l + 1]
        res = _mix_mlp(x, outs, stats, gb, u, yc, conv_w[l], gg[:A_WIDTH].reshape(1, A_WIDTH),
                       gg[A_WIDTH:c_start].reshape(1, CONV_CH), w_o_l,
                       g_mlp[l].reshape(1, D_MODEL), w_ff_in[l].astype(BF16), w_ff_out[l].astype(BF16),
                       g_next.reshape(1, D_MODEL), final=final)
        x = res[0]
        h = None if final else res[1]
    return x
```

```python
import functools

import jax
import jax.numpy as jnp
import numpy as np
from jax import lax
from jax.experimental import pallas as pl
from jax.experimental.pallas import tpu as pltpu

D_MODEL = 1024
HEAD_DIM = 64
A_HEADS = 6
DILATIONS = (1, 4, 16)
A_MAX_DIST = 128
CONV_CH = 256
CONV_K = 3
C_Q_HEADS = 6
C_KV_HEADS = 2
C_GROUP = C_Q_HEADS // C_KV_HEADS
C_MAX_DIST = 127
BLOCK = 128
D_FF = 4 * D_MODEL
EPS = 1e-6

A_WIDTH = A_HEADS * HEAD_DIM
C_WIDTH = C_Q_HEADS * HEAD_DIM
KV_WIDTH = C_KV_HEADS * HEAD_DIM
MIX_WIDTH = A_WIDTH + CONV_CH + C_WIDTH
IN_NAMES = ("qa", "ka", "va", "gb", "gc", "xb", "qc", "kc", "vc")
IN_SPLITS = (A_WIDTH, A_WIDTH, A_WIDTH, CONV_CH, CONV_CH, CONV_CH, C_WIDTH, KV_WIDTH, KV_WIDTH)
IN_WIDTH = sum(IN_SPLITS)
IN_START = dict(zip(IN_NAMES, np.cumsum((0,) + IN_SPLITS[:-1]).tolist()))
LANES = 128
SUBLANES = 8
A_GROUPS = A_WIDTH // LANES
CONV_TAIL = 16
C_HEAD_ORDER = (0, 3, 1, 4, 2, 5)
ATTN_BLOCKS = 16
LOG2E = 1.4426950408889634
L_LANE = 64

PROJ_TOKENS = 1024
PROJ_PART = 256
PROJ_CHUNK = 512
PROJ_GROUPS = ([(name, g) for name in ("qa", "ka", "va") for g in range(3)]
               + [("gb", 0), ("gb", 1), ("qc", 0), ("gc", 0), ("xb", 0), ("gc", 1), ("xb", 1),
                  ("qc", 1), ("qc", 2), ("kc", 0), ("vc", 0)])
MLP_TOKENS = 512
MLP_PART = 256
FF_CHUNK = 1024
VMEM_LIMIT = 56 * 1024 * 1024

NEG = -1e30
BF16 = jnp.bfloat16
F32 = jnp.float32


def _rms_normalize(t):
    return t * lax.rsqrt(jnp.mean(t * t, axis=-1, keepdims=True) + EPS)


def _dot(a, b):
    return jnp.dot(a, b, preferred_element_type=F32)


def _dot_nt(a, b):
    return lax.dot_general(a, b, (((1,), (1,)), ((), ())), preferred_element_type=F32)


def _lane_group(g):
    return slice(g * LANES, (g + 1) * LANES)


LOCAL_NAMES = ("qa", "ka", "va", "qc", "kc", "vc")
LOCAL_WIDTHS = (A_WIDTH, A_WIDTH, A_WIDTH, C_WIDTH, KV_WIDTH, KV_WIDTH)
CARRY_NAMES = ("ka", "va", "kc", "vc")
CARRY_WIDTHS = (A_WIDTH, A_WIDTH, KV_WIDTH, KV_WIDTH)


def _proj_attn_kernel(sink_ref, x_ref, g_ref, w_ref, bias_a_ref, bias_c_ref, perm_ref, gg_c_ref, *refs, normalized):
    o1_ref, st1_ref, yc_ref = refs[0:3]
    strided = (refs[3:6], refs[6:9])
    gb_ref, u_ref = refs[9:11]
    h_ref = refs[11]
    local = dict(zip(LOCAL_NAMES, refs[12:18]))
    carry = dict(zip(CARRY_NAMES, refs[18:22]))
    tm = x_ref.shape[1]
    blocks = tm // BLOCK

    @pl.when(pl.program_id(1) == 0)
    def _():
        for ref in carry.values():
            ref[...] = jnp.zeros(ref.shape, ref.dtype)

    parts = [slice(i * PROJ_PART, (i + 1) * PROJ_PART) for i in range(tm // PROJ_PART)]

    if not normalized:
        for part in parts:
            h_ref[part, :] = (_rms_normalize(x_ref[0, part, :]) * g_ref[...]).astype(BF16)

    def lhs(part):
        return x_ref[0, part, :] if normalized else h_ref[part, :]

    per_chunk = PROJ_CHUNK // LANES
    for c in range(IN_WIDTH // PROJ_CHUNK):
        for part in parts:
            z = _dot(lhs(part), w_ref[:, c * PROJ_CHUNK:(c + 1) * PROJ_CHUNK])
            gates = {}
            for k in range(per_chunk):
                name, g = PROJ_GROUPS[c * per_chunk + k]
                zk = z[:, _lane_group(k)]
                if name == "gc":
                    gates[g] = zk
                elif name == "xb":
                    u_ref[0, part, _lane_group(g)] = (gates.pop(g) * zk).astype(BF16)
                elif name == "gb":
                    gb_ref[0, part, _lane_group(g)] = zk.astype(BF16)
                else:
                    local[name][0, part, _lane_group(g)] = zk.astype(BF16)

    for i, part in enumerate(parts):
        qkv = jnp.concatenate([local[name][0, part, :] for name in ("qa", "ka", "va")], axis=1)
        for d, dil in enumerate(DILATIONS[1:]):
            n = PROJ_PART // dil
            moved = _dot(perm_ref[d], qkv).astype(BF16)
            for r in range(dil):
                for t in range(3):
                    strided[d][t][0, r, i * n:(i + 1) * n, :] = moved[r * n:(r + 1) * n, t * A_WIDTH:(t + 1) * A_WIDTH]

    lane = lax.broadcasted_iota(jnp.int32, (BLOCK, LANES), 1)
    first_bias = jnp.minimum(pl.program_id(1), 1)
    upper = lax.broadcasted_iota(jnp.int32, (2 * BLOCK, 1), 0) < BLOCK
    sinks = [jnp.where(upper, sink_ref[C_HEAD_ORDER[2 * g]], sink_ref[C_HEAD_ORDER[2 * g + 1]]) * LOG2E
             for g in range(C_WIDTH // LANES)]
    kv_lanes = slice(0, KV_WIDTH)
    units = [(kind, jb, g) for jb in range(blocks) for kind in ("a", "c") for g in range(A_GROUPS)]
    tiles, c_parts = {}, {}

    def scores_of(unit):
        kind, jb, g = unit
        bias_ref = bias_a_ref if kind == "a" else bias_c_ref
        bias = bias_ref[first_bias] if jb == 0 else bias_ref[1]
        if kind == "a":
            keys = _keys_of_block(carry["ka"], local["ka"], 0, jb, _lane_group(g))
            return _pair_scores(local["qa"][0, _block_rows(jb), _lane_group(g)], keys, bias)
        keys = _keys_of_block(carry["kc"], local["kc"], 0, jb, kv_lanes)
        return _pair_scores(local["qc"][0, _block_rows(jb), _lane_group(g)], keys, bias)

    def probs_of(unit, s):
        kind, _, g = unit
        return _pair_probs(s, None if kind == "a" else sinks[g])

    def finish(unit, probs):
        kind, jb, g = unit
        p, m = probs
        if kind == "a":
            r = _pair_values(p, _keys_of_block(carry["va"], local["va"], 0, jb, _lane_group(g)))
            o1_ref[0, _block_rows(jb), _lane_group(g)] = _both_heads(r[:, :LANES]).astype(o1_ref.dtype)
            tile = tiles.pop(jb, None)
            if tile is None:
                tile = jnp.where(lane >= L_LANE, 1.0, 0.0)
            for half in range(2):
                tile = jnp.where(lane == 2 * g + half, m[_block_rows(half)], tile)
                tile = jnp.where(lane == L_LANE + 2 * g + half, r[_block_rows(half), LANES:], tile)
            if g == A_GROUPS - 1:
                st1_ref[0, _block_rows(jb), :] = tile
            else:
                tiles[jb] = tile
        else:
            r = _pair_values(p, _keys_of_block(carry["vc"], local["vc"], 0, jb, kv_lanes))
            denom = r[:, LANES:] + jnp.exp2(sinks[g] - m)
            c_parts.setdefault(jb, []).append(_both_heads(r[:, :LANES]) / _both_heads(denom))
            if g == A_GROUPS - 1:
                oc = jnp.concatenate(c_parts.pop(jb), axis=1)
                yc_ref[0, _block_rows(jb), :] = (_rms_normalize(oc) * gg_c_ref[...]).astype(yc_ref.dtype)

    _pipeline(units, scores_of, probs_of, finish)

    for name in CARRY_NAMES:
        carry[name][0] = local[name][0, _block_rows(blocks - 1), :]


def _residue_permutations():
    mats = np.zeros((len(DILATIONS) - 1, PROJ_PART, PROJ_PART), np.float32)
    for d, dil in enumerate(DILATIONS[1:]):
        n = PROJ_PART // dil
        for r in range(dil):
            mats[d, r * n + np.arange(n), dil * np.arange(n) + r] = 1.0
    return jnp.asarray(mats, BF16)


def _proj_attn(x, g, w, sinks, gg_c, normalized):
    batch, seq, _ = x.shape
    tm = PROJ_TOKENS

    def rows(width):
        return pl.BlockSpec((1, tm, width), lambda b, i: (b, i, 0))

    def sub(dil):
        return pl.BlockSpec((1, dil, tm // dil, A_WIDTH), lambda b, i: (b, 0, i, 0))

    def const(shape):
        return pl.BlockSpec(shape, lambda b, i: (0,) * len(shape), pipeline_mode=pl.Buffered(1))

    out_specs = [rows(A_WIDTH), rows(LANES), rows(C_WIDTH)]
    out_shape = [jax.ShapeDtypeStruct((batch, seq, A_WIDTH), BF16), jax.ShapeDtypeStruct((batch, seq, LANES), F32),
                 jax.ShapeDtypeStruct((batch, seq, C_WIDTH), BF16)]
    for dil in DILATIONS[1:]:
        out_specs += [sub(dil)] * 3
        out_shape += [jax.ShapeDtypeStruct((batch, dil, seq // dil, A_WIDTH), BF16)] * 3
    out_specs += [rows(CONV_CH)] * 2
    out_shape += [jax.ShapeDtypeStruct((batch, seq, CONV_CH), BF16)] * 2
    scratch = ([pltpu.VMEM((tm, D_MODEL), BF16)]
               + [pltpu.VMEM((1, tm, wd), BF16) for wd in LOCAL_WIDTHS]
               + [pltpu.VMEM((1, BLOCK, wd), BF16) for wd in CARRY_WIDTHS])
    bias_spec = const((2, 2 * BLOCK, 2 * BLOCK))
    return pl.pallas_call(
        functools.partial(_proj_attn_kernel, normalized=normalized),
        grid=(batch, seq // tm),
        in_specs=[pl.BlockSpec(memory_space=pltpu.SMEM), rows(D_MODEL), const((1, D_MODEL)),
                  const((D_MODEL, IN_WIDTH)), bias_spec, bias_spec,
                  const((len(DILATIONS) - 1, PROJ_PART, PROJ_PART)), const((1, C_WIDTH))],
        out_specs=out_specs,
        out_shape=out_shape,
        scratch_shapes=scratch,
        compiler_params=pltpu.CompilerParams(
            dimension_semantics=("parallel", "arbitrary"), vmem_limit_bytes=VMEM_LIMIT),
        name="proj_attn",
    )(sinks, x, g, w, _band_bias(A_MAX_DIST), _band_bias(C_MAX_DIST), _residue_permutations(), gg_c)


def _band_bias(max_dist):
    qi = np.arange(2 * BLOCK)[:, None] % BLOCK
    kj = np.arange(2 * BLOCK)[None, :]
    dist = BLOCK + qi - kj
    band = (dist >= 0) & (dist <= max_dist)
    first = band & (kj >= BLOCK)
    return jnp.asarray(np.where(np.stack([first, band]), 0.0, NEG), F32)


def _low_half():
    return lax.broadcasted_iota(jnp.int32, (BLOCK, LANES), 1) < HEAD_DIM


def _pair_scores(q2, k2, bias):
    low = _low_half()
    zero = jnp.zeros_like(q2)
    stacked = jnp.concatenate([jnp.where(low, q2, zero), jnp.where(low, zero, q2)], axis=0)
    return _dot_nt(stacked, k2) + bias


def _pair_probs(s, sink=None):
    m = jnp.max(s, axis=-1, keepdims=True)
    if sink is not None:
        m = jnp.maximum(m, sink)
    return jnp.exp2(s - m).astype(BF16), m


def _pair_values(p, v2):
    ones = jnp.ones((2 * BLOCK, LANES), BF16)
    return _dot(p, jnp.concatenate([v2, ones], axis=1))


def _both_heads(t):
    return jnp.where(_low_half(), t[:BLOCK], t[BLOCK:])


def _pipeline(units, scores_of, probs_of, finish):
    n = len(units)
    scores, probs = {}, {}
    for t in range(n + 2):
        if t >= 2:
            finish(units[t - 2], probs.pop(t - 2))
        if 1 <= t <= n:
            probs[t - 1] = probs_of(units[t - 1], scores.pop(t - 1))
        if t < n:
            scores[t] = scores_of(units[t])


def _block_rows(jb):
    return slice(jb * BLOCK, (jb + 1) * BLOCK)


def _keys_of_block(prev_ref, cur_ref, sq, jb, cols):
    before = prev_ref[sq, :, cols] if jb == 0 else cur_ref[sq, _block_rows(jb - 1), cols]
    return jnp.concatenate([before, cur_ref[sq, _block_rows(jb), cols]], axis=0)


def _dilated_attn_kernel(bias_ref, q_ref, kp_ref, kc_ref, vp_ref, vc_ref, o_ref, st_ref, *, seqs, blocks):
    lane = lax.broadcasted_iota(jnp.int32, (BLOCK, LANES), 1)
    first_bias = jnp.minimum(pl.program_id(1), 1)
    units = [(sq, jb, g) for sq in range(seqs) for jb in range(blocks) for g in range(A_GROUPS)]
    tiles = {}

    def scores_of(unit):
        sq, jb, g = unit
        bias = bias_ref[first_bias] if jb == 0 else bias_ref[1]
        return _pair_scores(q_ref[sq, _block_rows(jb), _lane_group(g)],
                            _keys_of_block(kp_ref, kc_ref, sq, jb, _lane_group(g)), bias)

    def probs_of(unit, s):
        return _pair_probs(s)

    def finish(unit, probs):
        sq, jb, g = unit
        p, m = probs
        r = _pair_values(p, _keys_of_block(vp_ref, vc_ref, sq, jb, _lane_group(g)))
        o_ref[sq, _block_rows(jb), _lane_group(g)] = _both_heads(r[:, :LANES]).astype(o_ref.dtype)
        tile = tiles.pop((sq, jb), None)
        if tile is None:
            tile = jnp.where(lane >= L_LANE, 1.0, 0.0)
        for half in range(2):
            tile = jnp.where(lane == 2 * g + half, m[_block_rows(half)], tile)
            tile = jnp.where(lane == L_LANE + 2 * g + half, r[_block_rows(half), LANES:], tile)
        if g == A_GROUPS - 1:
            st_ref[sq, _block_rows(jb), :] = tile
        else:
            tiles[(sq, jb)] = tile

    _pipeline(units, scores_of, probs_of, finish)


def _dilated_attn(q, k, v):
    n_seq, length, _ = q.shape
    blocks = min(ATTN_BLOCKS, length // BLOCK)
    seqs = ATTN_BLOCKS // blocks
    cur = pl.BlockSpec((seqs, blocks * BLOCK, A_WIDTH), lambda s, i: (s, i, 0))
    prev = pl.BlockSpec((seqs, BLOCK, A_WIDTH), lambda s, i: (s, jnp.maximum(i * blocks - 1, 0), 0))
    stat = pl.BlockSpec((seqs, blocks * BLOCK, LANES), lambda s, i: (s, i, 0))
    return pl.pallas_call(
        functools.partial(_dilated_attn_kernel, seqs=seqs, blocks=blocks),
        grid=(n_seq // seqs, length // (blocks * BLOCK)),
        in_specs=[pl.BlockSpec((2, 2 * BLOCK, 2 * BLOCK), lambda s, i: (0, 0, 0)),
                  cur, prev, cur, prev, cur],
        out_specs=[cur, stat],
        out_shape=[jax.ShapeDtypeStruct((n_seq, length, A_WIDTH), BF16),
                   jax.ShapeDtypeStruct((n_seq, length, LANES), F32)],
        compiler_params=pltpu.CompilerParams(
            dimension_semantics=("parallel", "arbitrary"), vmem_limit_bytes=VMEM_LIMIT),
        name="dilated_attn",
    )(_band_bias(A_MAX_DIST), q, k, k, v, v)


def _expand_heads(w):
    head_of_col = lax.broadcasted_iota(jnp.int32, (LANES, A_WIDTH), 1) // HEAD_DIM
    lane = lax.broadcasted_iota(jnp.int32, (LANES, A_WIDTH), 0)
    return _dot(w.astype(BF16), (head_of_col == lane).astype(BF16))


def _gated_conv(gb_ref, u_ref, u_tail_ref, cw_ref, part):
    u_cur = u_ref[0, part, :].astype(F32)
    if part.start == 0:
        u_prev = jnp.where(pl.program_id(1) > 0, u_tail_ref[0].astype(F32), 0.0)
    else:
        u_prev = u_ref[0, part.start - CONV_TAIL:part.start, :].astype(F32)
    head_rows = lax.broadcasted_iota(jnp.int32, (SUBLANES, CONV_CH), 0)
    acc = cw_ref[CONV_K - 1:CONV_K, :] * u_cur
    for shift in range(1, CONV_K):
        shifted = pltpu.roll(u_cur, shift, 0)
        from_prev = pltpu.roll(u_prev, shift, 0)[:SUBLANES]
        head = jnp.where(head_rows < shift, from_prev, shifted[:SUBLANES])
        shifted = jnp.concatenate([head, shifted[SUBLANES:]], axis=0)
        acc = acc + cw_ref[CONV_K - 1 - shift:CONV_K - shift, :] * shifted
    return gb_ref[0, part, :].astype(F32) * acc


def _mix_mlp_kernel(x_ref, o1_ref, o2_ref, o3_ref, s1_ref, s2_ref, s3_ref,
                    gb_ref, u_ref, u_tail_ref, yc_ref, cw_ref, gg_a_ref, gg_b_ref, wo_ref, g_mlp_ref, w1_ref, w2_ref,
                    g_next_ref, *refs, final):
    if final:
        out_ref, o_nat, s_nat, tmp_nat = refs
    else:
        out_ref, h_ref, o_nat, s_nat, tmp_nat = refs
    tm = x_ref.shape[1]
    quarter = tm // 4

    def to_token_order(dst, tmp, src4, src16):
        for b in range(4):
            dst[0][pl.ds(b, quarter, stride=4), :] = src4(b)
            for a in range(4):
                tmp[pl.ds(b * quarter + a, quarter // 4, stride=4), :] = src16(4 * a + b)
        for b in range(4):
            dst[1][pl.ds(b, quarter, stride=4), :] = tmp[b * quarter:(b + 1) * quarter, :]

    to_token_order((s_nat.at[0], s_nat.at[1]), tmp_nat, lambda r: s2_ref[0, r], lambda r: s3_ref[0, r])
    for g in range(A_GROUPS):
        to_token_order((o_nat.at[g], o_nat.at[A_GROUPS + g]), tmp_nat,
                       lambda r: o2_ref[0, r, :, _lane_group(g)].astype(F32),
                       lambda r: o3_ref[0, r, :, _lane_group(g)].astype(F32))

    parts = [slice(i * MLP_PART, (i + 1) * MLP_PART) for i in range(tm // MLP_PART)]

    def mixed_heads(part):
        outs = [o1_ref[0, part, :].astype(F32)] + [
            jnp.concatenate([o_nat[j * A_GROUPS + g, part, :] for g in range(A_GROUPS)], axis=1) for j in range(2)]
        tops = (s1_ref[0, part, :], s_nat[0, part, :], s_nat[1, part, :])
        sums = [pltpu.roll(t, LANES - L_LANE, 1) for t in tops]
        top = jnp.maximum(jnp.maximum(tops[0], tops[1]), tops[2])
        es = [jnp.exp2(m - top) for m in tops]
        inv = 1.0 / (es[0] * sums[0] + es[1] * sums[1] + es[2] * sums[2])
        is_head = lax.broadcasted_iota(jnp.int32, (MLP_PART, LANES), 1) < A_HEADS
        ya = jnp.zeros((MLP_PART, A_WIDTH), F32)
        for e, o in zip(es, outs):
            ya = ya + _expand_heads(jnp.where(is_head, e * inv, 0.0)) * o
        return (_rms_normalize(ya) * gg_a_ref[...]).astype(BF16)

    ya = [mixed_heads(part) for part in parts]
    yb = [(_rms_normalize(_gated_conv(gb_ref, u_ref, u_tail_ref, cw_ref, part)) * gg_b_ref[...]).astype(BF16)
          for part in parts]
    xs = [x_ref[0, part, :] + _dot(jnp.concatenate([a, b, yc_ref[0, part, :]], axis=1), wo_ref[...])
          for part, a, b in zip(parts, ya, yb)]
    hs = [(_rms_normalize(x) * g_mlp_ref[...]).astype(BF16) for x in xs]
    ffs = [jnp.zeros((MLP_PART, D_MODEL), F32) for _ in parts]
    for c in range(D_FF // FF_CHUNK):
        cols = slice(c * FF_CHUNK, (c + 1) * FF_CHUNK)
        for i, h in enumerate(hs):
            a = jnp.square(jnp.maximum(_dot(h, w1_ref[:, cols]), 0.0))
            ffs[i] = ffs[i] + _dot(a.astype(BF16), w2_ref[cols, :])
    for part, x, ff in zip(parts, xs, ffs):
        x = x + ff
        if final:
            out_ref[0, part, :] = _rms_normalize(x) * g_next_ref[...]
        else:
            out_ref[0, part, :] = x
            h_ref[0, part, :] = (_rms_normalize(x) * g_next_ref[...]).astype(BF16)


def _mix_mlp(x, outs, stats, gb, u, yc, conv_w, gg_a, gg_b, wo, g_mlp, w1, w2, g_next, final):
    batch, seq, _ = x.shape
    tm = MLP_TOKENS

    def rows(width):
        return pl.BlockSpec((1, tm, width), lambda b, i: (b, i, 0))

    def sub(dil, width):
        return pl.BlockSpec((1, dil, tm // dil, width), lambda b, i: (b, 0, i, 0))

    def const(shape):
        return pl.BlockSpec(shape, lambda b, i: (0, 0), pipeline_mode=pl.Buffered(1))

    tail = pl.BlockSpec((1, CONV_TAIL, CONV_CH), lambda b, i: (b, jnp.maximum(i * (tm // CONV_TAIL) - 1, 0), 0))
    return pl.pallas_call(
        functools.partial(_mix_mlp_kernel, final=final),
        grid=(batch, seq // tm),
        in_specs=[
            rows(D_MODEL), rows(A_WIDTH), sub(DILATIONS[1], A_WIDTH), sub(DILATIONS[2], A_WIDTH),
            rows(LANES), sub(DILATIONS[1], LANES), sub(DILATIONS[2], LANES),
            rows(CONV_CH), rows(CONV_CH), tail, rows(C_WIDTH),
            const((CONV_K, CONV_CH)), const((1, A_WIDTH)), const((1, CONV_CH)), const((MIX_WIDTH, D_MODEL)),
            const((1, D_MODEL)),
            const((D_MODEL, D_FF)), const((D_FF, D_MODEL)), const((1, D_MODEL)),
        ],
        out_specs=[rows(D_MODEL)] * (1 if final else 2),
        out_shape=[jax.ShapeDtypeStruct((batch, seq, D_MODEL), F32)] + (
            [] if final else [jax.ShapeDtypeStruct((batch, seq, D_MODEL), BF16)]),
        scratch_shapes=[pltpu.VMEM((2 * A_GROUPS, tm, LANES), F32), pltpu.VMEM((2, tm, LANES), F32),
                        pltpu.VMEM((tm, LANES), F32)],
        compiler_params=pltpu.CompilerParams(
            dimension_semantics=("parallel", "parallel"), vmem_limit_bytes=VMEM_LIMIT),
        name="mix_mlp",
    )(x, *outs, *stats, gb, u, u, yc, conv_w, gg_a, gg_b, wo, g_mlp, w1, w2, g_next)


def kernel(x, w_in, conv_w, sinks, g_mix, g_group, w_o, g_mlp, w_ff_in, w_ff_out, g_final):
    batch, seq, _ = x.shape
    depth = w_in.shape[0]
    qc_start = sum(IN_SPLITS[:6])
    c_start = A_WIDTH + CONV_CH
    q_scale = np.ones((IN_WIDTH,), np.float32)
    q_scale[0:A_WIDTH] = HEAD_DIM ** -0.5 * LOG2E
    q_scale[qc_start:qc_start + C_WIDTH] = HEAD_DIM ** -0.5 * LOG2E

    def reorder_c_heads(t, axis):
        heads = [lax.slice_in_dim(t, h * HEAD_DIM, (h + 1) * HEAD_DIM, axis=axis) for h in C_HEAD_ORDER]
        return jnp.concatenate(heads, axis=axis)

    def project_columns(w):
        w = (w * q_scale).astype(BF16)
        pieces = []
        for name, g in PROJ_GROUPS:
            if name == "qc":
                starts = [qc_start + h * HEAD_DIM for h in C_HEAD_ORDER[2 * g:2 * g + 2]]
                pieces += [lax.slice_in_dim(w, st, st + HEAD_DIM, axis=1) for st in starts]
            else:
                pieces.append(lax.slice_in_dim(w, IN_START[name] + g * LANES, IN_START[name] + (g + 1) * LANES, axis=1))
        return jnp.concatenate(pieces, axis=1)

    h = None
    for l in range(depth):
        final = l == depth - 1
        gg = g_group[l]
        (o1, st1, yc, q4, k4, v4, q16, k16, v16, gb, u) = _proj_attn(
            x if h is None else h, g_mix[l].reshape(1, D_MODEL), project_columns(w_in[l]),
            sinks[l].reshape(C_Q_HEADS), reorder_c_heads(gg[c_start:], 0).reshape(1, C_WIDTH),
            normalized=h is not None)

        outs, stats = [o1], [st1]
        for dil, (q, k, v) in zip(DILATIONS[1:], ((q4, k4, v4), (q16, k16, v16))):
            o, st = _dilated_attn(*(t.reshape(batch * dil, seq // dil, A_WIDTH) for t in (q, k, v)))
            outs.append(o.reshape(batch, dil, seq // dil, A_WIDTH))
            stats.append(st.reshape(batch, dil, seq // dil, LANES))

        w_o_b = w_o[l].astype(BF16)
        w_o_l = jnp.concatenate([w_o_b[:c_start], reorder_c_heads(w_o_b[c_start:], 0)], axis=0)
        g_next = g_final if final else g_mix[l + 1]
        res = _mix_mlp(x, outs, stats, gb, u, yc, conv_w[l], gg[:A_WIDTH].reshape(1, A_WIDTH),
                       gg[A_WIDTH:c_start].reshape(1, CONV_CH), w_o_l,
                       g_mlp[l].reshape(1, D_MODEL), w_ff_in[l].astype(BF16), w_ff_out[l].astype(BF16),
                       g_next.reshape(1, D_MODEL), final=final)
        x = res[0]
        h = None if final else res[1]
    return x
```
